```python
import jax, jax.numpy as jnp
from jax import lax
import numpy as np

D_MODEL = 1024
BATCH = 8
SEQ = 4096
DEPTH = 4

CTX_LEN = 256
GRID_W = 64

F_GROUPS = 4
F_GROUP_DIM = D_MODEL // 16
F_WIDTH = F_GROUPS * F_GROUP_DIM
G_HEADS = 4
G_HEAD_DIM = D_MODEL // 16
G_WIDTH = G_HEADS * G_HEAD_DIM
CHUNK = 128
A_HEADS = 8
A_NOPE = D_MODEL // 16
A_ROPE = D_MODEL // 32
A_V = D_MODEL // 16
A_QK = A_NOPE + A_ROPE
A_WIDTH = A_HEADS * A_V
Q_LORA = D_MODEL // 4
KV_LORA = D_MODEL // 8

MIX_WIDTH = F_WIDTH + G_WIDTH + A_WIDTH
IN_WIDTH = 2 * F_WIDTH + 3 * G_WIDTH + Q_LORA + KV_LORA + A_ROPE + A_WIDTH
ROPE_BASE = 10000.0
EPS = 1e-6
Q_BLOCK = 128

kernel_name = "hybrid_fourier_gmlp_mla_prefix_dit"


def _rmsnorm(t, g):
    tf = t.astype(jnp.float32)
    tf = tf * lax.rsqrt(jnp.mean(tf * tf, axis=-1, keepdims=True) + EPS)
    return (tf * g.astype(jnp.float32)).astype(t.dtype)


def _layernorm(t, g):
    tf = t.astype(jnp.float32)
    tf = tf - jnp.mean(tf, axis=-1, keepdims=True)
    tf = tf * lax.rsqrt(jnp.mean(tf * tf, axis=-1, keepdims=True) + EPS)
    return (tf * g.astype(jnp.float32)).astype(t.dtype)


def _split_in(p):
    sizes = (F_WIDTH, F_WIDTH, G_WIDTH, G_WIDTH, G_WIDTH, Q_LORA, KV_LORA, A_ROPE, A_WIDTH)
    parts = []
    start = 0
    for s in sizes:
        parts.append(p[..., start:start + s])
        start += s
    return parts


def _axial_rope(n):
    rows = n // GRID_W
    row = jnp.repeat(jnp.arange(rows, dtype=jnp.float32), GRID_W)
    col = jnp.tile(jnp.arange(GRID_W, dtype=jnp.float32), rows)
    half = A_ROPE // 2
    inv = ROPE_BASE ** (-jnp.arange(0, half, 2, dtype=jnp.float32) / half)
    ang_r = row[:, None] * inv[None, :]
    ang_c = col[:, None] * inv[None, :]
    ang = jnp.concatenate([ang_r, ang_r, ang_c, ang_c], axis=-1)
    return jnp.cos(ang), jnp.sin(ang)


def _apply_rope(t, rope):
    cos, sin = rope
    q = A_ROPE // 4
    x1, x2, x3, x4 = t[..., :q], t[..., q:2 * q], t[..., 2 * q:3 * q], t[..., 3 * q:]
    rot = jnp.concatenate([-x2, x1, -x4, x3], axis=-1)
    cos = cos[:, None, :].astype(t.dtype)
    sin = sin[:, None, :].astype(t.dtype)
    return t * cos + rot * sin


def _fourier(u):
    b, n, _ = u.shape
    ug = u.astype(jnp.float32).reshape(b, n, F_GROUPS, F_GROUP_DIM).transpose(0, 2, 1, 3)
    f = jnp.fft.fft2(ug, norm="ortho").real
    return f.transpose(0, 2, 1, 3).reshape(b, n, F_WIDTH).astype(u.dtype)


def _spatial_gate(u, v, ln_g, ws, bs):
    b, n, _ = v.shape
    vh = v.reshape(b, n // CHUNK, CHUNK, G_HEADS, G_HEAD_DIM)
    vh = _layernorm(vh, ln_g)
    mixed = jnp.einsum('hpq,bcqhd->bcphd', ws, vh) + bs.T[None, None, :, :, None]
    return u * mixed.reshape(b, n, G_WIDTH)


def _mla_qkv(c_q, c_kv, k_rope, q_a_g, w_uq, kv_a_g, w_ukv, q_norm_g, k_norm_g, rope):
    b, n, _ = c_q.shape
    q = (_rmsnorm(c_q, q_a_g) @ w_uq).reshape(b, n, A_HEADS, A_QK)
    kv = (_rmsnorm(c_kv, kv_a_g) @ w_ukv).reshape(b, n, A_HEADS, A_NOPE + A_V)
    k_nope, v = kv[..., :A_NOPE], kv[..., A_NOPE:]
    k_r = jnp.broadcast_to(k_rope[:, :, None, :], (b, n, A_HEADS, A_ROPE))
    k = jnp.concatenate([k_nope, k_r], axis=-1)
    q = _rmsnorm(q, q_norm_g)
    k = _rmsnorm(k, k_norm_g)
    if rope is not None:
        q = jnp.concatenate([q[..., :A_NOPE], _apply_rope(q[..., A_NOPE:], rope)], axis=-1)
        k = jnp.concatenate([k[..., :A_NOPE], _apply_rope(k[..., A_NOPE:], rope)], axis=-1)
    return q, k, v


def _branches(h, w_in, w_fmix, g_ln_g, g_ws, g_bs, q_a_g, w_uq, kv_a_g, w_ukv, q_norm_g, k_norm_g, rope):
    f_in, f_gate, g_u, g_v, g_gate, c_q, c_kv, k_rope, a_gate = _split_in(h @ w_in)
    f_out = (_fourier(f_in) @ w_fmix) * jax.nn.silu(f_gate)
    g_out = _spatial_gate(g_u, g_v, g_ln_g, g_ws, g_bs) * jax.nn.silu(g_gate)
    q, k, v = _mla_qkv(c_q, c_kv, k_rope, q_a_g, w_uq, kv_a_g, w_ukv, q_norm_g, k_norm_g, rope)
    return f_out, g_out, q, k, v, a_gate


def _attend(q, k, v):
    s = jnp.einsum('bqhd,bkhd->bhqk', q, k).astype(jnp.float32) * (A_QK ** -0.5)
    p = jax.nn.softmax(s, axis=-1).astype(v.dtype)
    return jnp.einsum('bhqk,bkhd->bqhd', p, v)


def _blocked_attention(q, k, v):
    b, n, h, d = q.shape
    nb = n // Q_BLOCK
    qb = q.reshape(b, nb, Q_BLOCK, h, d).transpose(1, 0, 2, 3, 4)
    o = lax.map(lambda blk: _attend(blk, k, v), qb)
    return o.transpose(1, 0, 2, 3, 4).reshape(b, n, h * A_V)


def setup_inputs(seed: int = 0) -> dict:
    key = jax.random.key(seed)
    ks = jax.random.split(key, 24)
    f32 = jnp.float32
    nrm = lambda k, shape, s: jax.random.normal(k, shape, f32) * s
    gain = lambda k, shape: 1.0 + 0.02 * jax.random.normal(k, shape, f32)
    return {
        "x": nrm(ks[0], (BATCH, SEQ, D_MODEL), 1.0),
        "c": nrm(ks[1], (BATCH, D_MODEL), 1.0),
        "ctx": nrm(ks[2], (BATCH, CTX_LEN, D_MODEL), 1.0),
        "c_ctx": nrm(ks[3], (D_MODEL,), 1.0),
        "w_mod": nrm(ks[4], (DEPTH, D_MODEL, 3 * D_MODEL), 0.5 * D_MODEL ** -0.5),
        "b_mod": nrm(ks[5], (DEPTH, 3 * D_MODEL), 0.01),
        "norm_g": gain(ks[6], (DEPTH, D_MODEL)),
        "w_in": nrm(ks[7], (DEPTH, D_MODEL, IN_WIDTH), D_MODEL ** -0.5),
        "w_fmix": nrm(ks[8], (DEPTH, F_WIDTH, F_WIDTH), F_WIDTH ** -0.5),
        "g_ln_g": gain(ks[9], (DEPTH, G_HEAD_DIM)),
        "g_ws": nrm(ks[10], (DEPTH, G_HEADS, CHUNK, CHUNK), CHUNK ** -0.5),
        "g_bs": 1.0 + nrm(ks[11], (DEPTH, G_HEADS, CHUNK), 0.01),
        "q_a_g": gain(ks[12], (DEPTH, Q_LORA)),
        "w_uq": nrm(ks[13], (DEPTH, Q_LORA, A_HEADS * A_QK), Q_LORA ** -0.5),
        "kv_a_g": gain(ks[14], (DEPTH, KV_LORA)),
        "w_ukv": nrm(ks[15], (DEPTH, KV_LORA, A_HEADS * (A_NOPE + A_V)), KV_LORA ** -0.5),
        "q_norm_g": gain(ks[16], (DEPTH, A_QK)),
        "k_norm_g": gain(ks[17], (DEPTH, A_QK)),
        "w_out": nrm(ks[18], (DEPTH, MIX_WIDTH, D_MODEL), MIX_WIDTH ** -0.5),
    }


def reference(x, c, ctx, c_ctx, w_mod, b_mod, norm_g, w_in, w_fmix, g_ln_g, g_ws, g_bs,
              q_a_g, w_uq, kv_a_g, w_ukv, q_norm_g, k_norm_g, w_out):
    n = x.shape[1]
    rope = _axial_rope(n)
    y = ctx
    silu_c = jax.nn.silu(c)
    silu_cc = jax.nn.silu(c_ctx)
    for l in range(DEPTH):
        shift, scale, gate = jnp.split(silu_c @ w_mod[l] + b_mod[l], 3, axis=-1)
        shift_c, scale_c, gate_c = jnp.split(silu_cc @ w_mod[l] + b_mod[l], 3, axis=-1)
        hx = _rmsnorm(x, norm_g[l]) * (1.0 + scale[:, None, :]) + shift[:, None, :]
        hy = _rmsnorm(y, norm_g[l]) * (1.0 + scale_c) + shift_c
        params = (w_in[l], w_fmix[l], g_ln_g[l], g_ws[l], g_bs[l], q_a_g[l], w_uq[l],
                  kv_a_g[l], w_ukv[l], q_norm_g[l], k_norm_g[l])
        fy, gy, qy, ky, vy, ay = _branches(hy, *params, None)
        fx, gx, qx, kx, vx, ax = _branches(hx, *params, rope)
        k_all = jnp.concatenate([ky, kx], axis=1)
        v_all = jnp.concatenate([vy, vx], axis=1)
        ox = _blocked_attention(qx, k_all, v_all) * jax.nn.silu(ax)
        x = x + gate[:, None, :] * (jnp.concatenate([fx, gx, ox], axis=-1) @ w_out[l])
        if l < DEPTH - 1:
            b = y.shape[0]
            oy = _attend(qy, ky, vy).reshape(b, y.shape[1], A_WIDTH) * jax.nn.silu(ay)
            y = y + gate_c * (jnp.concatenate([fy, gy, oy], axis=-1) @ w_out[l])
    return x
```

```python
import functools
import math

import jax
import jax.numpy as jnp
from jax import lax
from jax.experimental import pallas as pl
from jax.experimental.pallas import tpu as pltpu

F32 = jnp.float32
BF16 = jnp.bfloat16

LANES = 128
VMEM_LIMIT_BYTES = 56 * 1024 * 1024

GRID_W = 64
F_GROUPS = 4
F_GROUP_DIM = 64
F_WIDTH = F_GROUPS * F_GROUP_DIM
G_HEADS = 4
G_HEAD_DIM = 64
G_WIDTH = G_HEADS * G_HEAD_DIM
CHUNK = 128
A_HEADS = 8
A_NOPE = 64
A_ROPE = 32
A_V = 64
A_QK = A_NOPE + A_ROPE
A_WIDTH = A_HEADS * A_V
Q_LORA = 256
KV_LORA = 128
ROPE_BASE = 10000.0
EPS = 1e-6
ONES_ROWS = 16

OFF_FIN = 0
OFF_FGATE = OFF_FIN + F_WIDTH
OFF_GU = OFF_FGATE + F_WIDTH
OFF_GV = OFF_GU + G_WIDTH
OFF_GGATE = OFF_GV + G_WIDTH
OFF_CQ = OFF_GGATE + G_WIDTH
OFF_CKV = OFF_CQ + Q_LORA
OFF_KR = OFF_CKV + KV_LORA
OFF_AGATE = OFF_KR + LANES
IN_P = OFF_AGATE + A_WIDTH


def _silu(t):
    return t * (1.0 / (1.0 + jnp.exp(-t)))


def _dot(a, b):
    return jnp.dot(a, b, preferred_element_type=F32)


def _dot_nt(a, b):
    return lax.dot_general(a, b, (((1,), (1,)), ((), ())), preferred_element_type=F32)


def _dot_hilo(t, w):
    hi = t.astype(BF16)
    lo = (t - hi.astype(F32)).astype(BF16)
    return _dot(hi, w) + _dot(lo, w)


def _mod_kernel(cc_ref, w_ref, b_ref, o_ref):
    o_ref[0] = lax.dot_general(cc_ref[...], w_ref[0], (((1,), (0,)), ((), ())),
                               precision=lax.Precision.HIGHEST,
                               preferred_element_type=F32) + b_ref[0]


def _modulation(cc, w_mod, b_mod):
    depth, d, d3 = w_mod.shape
    rows = cc.shape[0]
    tn = 1024
    return pl.pallas_call(
        _mod_kernel,
        grid=(depth, d3 // tn),
        in_specs=[pl.BlockSpec((rows, d), lambda l, j: (0, 0)),
                  pl.BlockSpec((1, d, tn), lambda l, j: (l, 0, j)),
                  pl.BlockSpec((1, 1, tn), lambda l, j: (l, 0, j))],
        out_specs=pl.BlockSpec((1, rows, tn), lambda l, j: (l, 0, j)),
        out_shape=jax.ShapeDtypeStruct((depth, rows, d3), F32),
        compiler_params=pltpu.CompilerParams(vmem_limit_bytes=VMEM_LIMIT_BYTES),
        name="modulation",
    )(cc, w_mod, b_mod.reshape(depth, 1, d3))


def _head_norm_rope(t, tab1, tab2, lane_mask):
    ss = jnp.sum(t * t * lane_mask, axis=-1, keepdims=True)
    u = t * lax.rsqrt(ss * (1.0 / A_QK) + EPS)
    return u * tab1 + pltpu.roll(u * tab2, LANES - A_ROPE, 1)


def _in_proj_kernel(x_ref, sc_ref, sh_ref, ng_ref, win_ref, bc_ref, bs_ref, jm_ref, gln_ref,
                    ws_ref, bsb_ref, qag_ref, wuq_ref, kvag_ref, wuk_ref, wuv_ref,
                    tq1_ref, tq2_ref, tk1_ref, tk2_ref,
                    fa_ref, fb_ref, sfg_ref, gout_ref, q_ref, k_ref, vt_ref, sag_ref):
    tm = x_ref.shape[1]
    x = x_ref[0]
    xn = x * lax.rsqrt(jnp.mean(x * x, axis=-1, keepdims=True) + EPS) * ng_ref[...]
    hb = (xn * (1.0 + sc_ref[0]) + sh_ref[0]).astype(BF16)

    f_in = _dot(hb, win_ref[:, OFF_FIN:OFF_FIN + F_WIDTH]).astype(BF16)
    fa_ref[0] = _dot(f_in, bc_ref[...]).astype(BF16)
    fb_ref[0] = _dot(f_in, bs_ref[...]).astype(BF16)
    sfg_ref[0] = _silu(_dot(hb, win_ref[:, OFF_FGATE:OFF_FGATE + F_WIDTH]))

    g_u = _dot(hb, win_ref[:, OFF_GU:OFF_GU + G_WIDTH])
    g_v = _dot(hb, win_ref[:, OFF_GV:OFF_GV + G_WIDTH])
    g_gate = _dot(hb, win_ref[:, OFF_GGATE:OFF_GGATE + G_WIDTH])
    jm = jm_ref[...]
    dv = g_v - _dot_hilo(g_v, jm)
    vln = (dv * lax.rsqrt(_dot_hilo(dv * dv, jm) + EPS) * gln_ref[...])
    head_of_lane = lax.broadcasted_iota(jnp.int32, (CHUNK, G_WIDTH), 1) // G_HEAD_DIM
    for c in range(tm // CHUNK):
        rows = slice(c * CHUNK, (c + 1) * CHUNK)
        vc = vln[rows]
        stacked = jnp.concatenate(
            [jnp.where(head_of_lane == h, vc, 0.0).astype(BF16) for h in range(G_HEADS)], axis=0)
        mixed = _dot(ws_ref[...], stacked) + bsb_ref[...]
        gout_ref[0, rows, :] = (g_u[rows] * mixed * _silu(g_gate[rows])).astype(BF16)

    lane = lax.broadcasted_iota(jnp.int32, (1, LANES), 1)
    lane_mask = (lane < A_QK).astype(F32)
    c_q = _dot(hb, win_ref[:, OFF_CQ:OFF_CQ + Q_LORA])
    cqn = (c_q * lax.rsqrt(jnp.mean(c_q * c_q, axis=-1, keepdims=True) + EPS) * qag_ref[...]).astype(BF16)
    tq1 = tq1_ref[...]
    tq2 = tq2_ref[...]
    for h in range(A_HEADS):
        t = _dot(cqn, wuq_ref[:, h * LANES:(h + 1) * LANES])
        q_ref[0, h] = _head_norm_rope(t, tq1, tq2, lane_mask).astype(BF16)

    c_kv = _dot(hb, win_ref[:, OFF_CKV:OFF_CKV + KV_LORA])
    ckvn = (c_kv * lax.rsqrt(jnp.mean(c_kv * c_kv, axis=-1, keepdims=True) + EPS) * kvag_ref[...]).astype(BF16)
    kr = _dot(hb, win_ref[:, OFF_KR:OFF_KR + LANES])
    tk1 = tk1_ref[...]
    tk2 = tk2_ref[...]
    for h in range(A_HEADS):
        t = _dot(ckvn, wuk_ref[:, h * LANES:(h + 1) * LANES]) + kr
        k_ref[0, h] = _head_norm_rope(t, tk1, tk2, lane_mask).astype(BF16)
    v = _dot(ckvn, wuv_ref[...])
    vt_ref[0] = v.T.astype(BF16).reshape(A_HEADS, A_V, tm)

    sag_ref[0] = _silu(_dot(hb, win_ref[:, OFF_AGATE:OFF_AGATE + A_WIDTH]))


def _in_proj(x, scale, shift, lw, tabs, tm):
    b, n, d = x.shape
    mod_map = (lambda bi, i: (bi, 0, 0)) if scale.shape[0] > 1 else (lambda bi, i: (0, 0, 0))
    const2 = lambda bi, i: (0, 0)
    row_tab = lambda bi, i: (i, 0)
    tok = lambda bi, i: (bi, i, 0)
    full = lambda a: pl.BlockSpec(a.shape, const2)
    in_specs = [
        pl.BlockSpec((1, tm, d), tok),
        pl.BlockSpec((1, 1, d), mod_map),
        pl.BlockSpec((1, 1, d), mod_map),
        full(lw["norm_g"]), full(lw["w_in"]), full(lw["bc"]), full(lw["bs"]), full(lw["jm"]),
        full(lw["g_ln_g"]), full(lw["ws_cat"]), full(lw["bs_tab"]), full(lw["q_a_g"]),
        full(lw["w_uq"]), full(lw["kv_a_g"]), full(lw["w_uk"]), full(lw["w_uv"]),
        pl.BlockSpec((tm, LANES), row_tab), pl.BlockSpec((tm, LANES), row_tab),
        pl.BlockSpec((tm, LANES), row_tab), pl.BlockSpec((tm, LANES), row_tab),
    ]
    out_shape = [
        jax.ShapeDtypeStruct((b, n, F_WIDTH), BF16),
        jax.ShapeDtypeStruct((b, n, F_WIDTH), BF16),
        jax.ShapeDtypeStruct((b, n, F_WIDTH), F32),
        jax.ShapeDtypeStruct((b, n, G_WIDTH), BF16),
        jax.ShapeDtypeStruct((b, A_HEADS, n, LANES), BF16),
        jax.ShapeDtypeStruct((b, A_HEADS, n, LANES), BF16),
        jax.ShapeDtypeStruct((b, A_HEADS, A_V, n), BF16),
        jax.ShapeDtypeStruct((b, n, A_WIDTH), F32),
    ]
    head_tok = lambda bi, i: (bi, 0, i, 0)
    out_specs = [
        pl.BlockSpec((1, tm, F_WIDTH), tok),
        pl.BlockSpec((1, tm, F_WIDTH), tok),
        pl.BlockSpec((1, tm, F_WIDTH), tok),
        pl.BlockSpec((1, tm, G_WIDTH), tok),
        pl.BlockSpec((1, A_HEADS, tm, LANES), head_tok),
        pl.BlockSpec((1, A_HEADS, tm, LANES), head_tok),
        pl.BlockSpec((1, A_HEADS, A_V, tm), lambda bi, i: (bi, 0, 0, i)),
        pl.BlockSpec((1, tm, A_WIDTH), tok),
    ]
    return pl.pallas_call(
        _in_proj_kernel,
        grid=(b, n // tm),
        in_specs=in_specs,
        out_specs=out_specs,
        out_shape=out_shape,
        compiler_params=pltpu.CompilerParams(
            dimension_semantics=("arbitrary", "arbitrary"), vmem_limit_bytes=VMEM_LIMIT_BYTES),
        name="in_proj",
    )(x, scale, shift, lw["norm_g"], lw["w_in"], lw["bc"], lw["bs"], lw["jm"], lw["g_ln_g"],
      lw["ws_cat"], lw["bs_tab"], lw["q_a_g"], lw["w_uq"], lw["kv_a_g"], lw["w_uk"], lw["w_uv"],
      tabs["q1"], tabs["q2"], tabs["k1"], tabs["k2"])


def _pos_dft_kernel(cn_ref, sn_ref, fa_ref, fb_ref, o_ref, *, norm):
    o_ref[0] = (_dot(cn_ref[...], fa_ref[0]) + _dot(sn_ref[...], fb_ref[0])) * norm


def _pos_dft(cos_n, nsin_n, fa, fb, tn):
    b, n, w = fa.shape
    return pl.pallas_call(
        functools.partial(_pos_dft_kernel, norm=1.0 / math.sqrt(n * F_GROUP_DIM)),
        grid=(n // tn, b),
        in_specs=[pl.BlockSpec((tn, n), lambda i, bi: (i, 0)),
                  pl.BlockSpec((tn, n), lambda i, bi: (i, 0)),
                  pl.BlockSpec((1, n, w), lambda i, bi: (bi, 0, 0)),
                  pl.BlockSpec((1, n, w), lambda i, bi: (bi, 0, 0))],
        out_specs=pl.BlockSpec((1, tn, w), lambda i, bi: (bi, i, 0)),
        out_shape=jax.ShapeDtypeStruct((b, n, w), F32),
        compiler_params=pltpu.CompilerParams(
            dimension_semantics=("arbitrary", "arbitrary"), vmem_limit_bytes=VMEM_LIMIT_BYTES),
        name="pos_dft",
    )(cos_n, nsin_n, fa, fb)


def _attention_kernel(*refs, seg_lens, heads_per_step):
    nseg = len(seg_lens)
    q_ref = refs[0]
    k_refs = refs[1:1 + nseg]
    vt_refs = refs[1 + nseg:1 + 2 * nseg]
    o_ref = refs[1 + 2 * nseg]
    k_scr, vt_scr = refs[2 + 2 * nseg:]

    @pl.when(pl.program_id(2) == 0)
    def _stage_keys():
        start = 0
        for s in range(nseg):
            k_scr[:, start:start + seg_lens[s], :] = k_refs[s][0]
            vt_scr[:, 0:A_V, start:start + seg_lens[s]] = vt_refs[s][0]
            start += seg_lens[s]
        vt_scr[:, A_V:, :] = jnp.ones((heads_per_step, ONES_ROWS, start), BF16)

    outs = []
    for h in range(heads_per_step):
        s_t = _dot_nt(k_scr[h], q_ref[0, h])
        m = jnp.max(s_t, axis=0, keepdims=True)
        p_t = jnp.exp2(s_t - m).astype(BF16)
        acc = _dot(vt_scr[h], p_t)
        outs.append(acc[0:A_V] / acc[A_V:A_V + 1])
    o_ref[0] = jnp.concatenate(outs, axis=0).T


def _attention(q, ks, vts, tq):
    b, heads, n, _ = q.shape
    seg_lens = tuple(int(k.shape[2]) for k in ks)
    keys = sum(seg_lens)
    hps = LANES // A_V
    qmap = lambda bi, hp, qi: (bi, hp, qi, 0)
    kvmap = lambda bi, hp, qi: (bi, hp, 0, 0)
    in_specs = [pl.BlockSpec((1, hps, tq, LANES), qmap)]
    in_specs += [pl.BlockSpec((1, hps, s, LANES), kvmap) for s in seg_lens]
    in_specs += [pl.BlockSpec((1, hps, A_V, s), kvmap) for s in seg_lens]
    return pl.pallas_call(
        functools.partial(_attention_kernel, seg_lens=seg_lens, heads_per_step=hps),
        grid=(b, heads // hps, n // tq),
        in_specs=in_specs,
        out_specs=pl.BlockSpec((1, tq, LANES), lambda bi, hp, qi: (bi, qi, hp)),
        out_shape=jax.ShapeDtypeStruct((b, n, heads * A_V), F32),
        scratch_shapes=[pltpu.VMEM((hps, keys, LANES), BF16),
                        pltpu.VMEM((hps, A_V + ONES_ROWS, keys), BF16)],
        compiler_params=pltpu.CompilerParams(
            dimension_semantics=("arbitrary", "arbitrary", "arbitrary"),
            vmem_limit_bytes=VMEM_LIMIT_BYTES),
        name="attention",
    )(q, *ks, *vts)


def _out_proj_kernel(x_ref, gate_ref, f_ref, sfg_ref, gout_ref, o_ref, sag_ref, wf_ref, wo_ref, y_ref):
    f_out = (_dot(f_ref[0].astype(BF16), wf_ref[...]) * sfg_ref[0]).astype(BF16)
    o_g = (o_ref[0] * sag_ref[0]).astype(BF16)
    y = (_dot(f_out, wo_ref[0:F_WIDTH, :])
         + _dot(gout_ref[0], wo_ref[F_WIDTH:F_WIDTH + G_WIDTH, :])
         + _dot(o_g, wo_ref[F_WIDTH + G_WIDTH:, :]))
    y_ref[0] = x_ref[0] + gate_ref[0] * y


def _out_proj(x, gate, f, sfg, gout, o, sag, w_fmix, w_out, tm):
    b, n, d = x.shape
    tok = lambda bi, i: (bi, i, 0)
    mod_map = (lambda bi, i: (bi, 0, 0)) if gate.shape[0] > 1 else (lambda bi, i: (0, 0, 0))
    const2 = lambda bi, i: (0, 0)
    return pl.pallas_call(
        _out_proj_kernel,
        grid=(b, n // tm),
        in_specs=[pl.BlockSpec((1, tm, d), tok),
                  pl.BlockSpec((1, 1, d), mod_map),
                  pl.BlockSpec((1, tm, F_WIDTH), tok),
                  pl.BlockSpec((1, tm, F_WIDTH), tok),
                  pl.BlockSpec((1, tm, G_WIDTH), tok),
                  pl.BlockSpec((1, tm, A_WIDTH), tok),
                  pl.BlockSpec((1, tm, A_WIDTH), tok),
                  pl.BlockSpec(w_fmix.shape, const2),
                  pl.BlockSpec(w_out.shape, const2)],
        out_specs=pl.BlockSpec((1, tm, d), tok),
        out_shape=jax.ShapeDtypeStruct((b, n, d), F32),
        compiler_params=pltpu.CompilerParams(
            dimension_semantics=("arbitrary", "arbitrary"), vmem_limit_bytes=VMEM_LIMIT_BYTES),
        name="out_proj",
    )(x, gate, f, sfg, gout, o, sag, w_fmix, w_out)


def _rot_perm_sign():
    q = A_ROPE // 4
    src, sign = [], []
    for j in range(A_ROPE):
        blk, r = divmod(j, q)
        if blk % 2 == 0:
            src.append((blk + 1) * q + r)
            sign.append(-1.0)
        else:
            src.append((blk - 1) * q + r)
            sign.append(1.0)
    return jnp.array(src, jnp.int32), jnp.array(sign, F32)


def _dft_tables(n):
    idx = (jnp.arange(n, dtype=jnp.int32)[:, None] * jnp.arange(n, dtype=jnp.int32)[None, :]) % n
    ang = idx.astype(F32) * (2.0 * math.pi / n)
    return jnp.cos(ang), jnp.sin(ang)


def _rope_cos_sin(n):
    rows = n // GRID_W
    row = jnp.repeat(jnp.arange(rows, dtype=F32), GRID_W)
    col = jnp.tile(jnp.arange(GRID_W, dtype=F32), rows)
    half = A_ROPE // 2
    inv = ROPE_BASE ** (-jnp.arange(0, half, 2, dtype=F32) / half)
    ang_r = row[:, None] * inv[None, :]
    ang_c = col[:, None] * inv[None, :]
    ang = jnp.concatenate([ang_r, ang_r, ang_c, ang_c], axis=-1)
    return jnp.cos(ang), jnp.sin(ang)


def _head_tables(gain, cos, sin, src, extra_scale):
    n = cos.shape[0]
    g_nope = jnp.broadcast_to(gain[:A_NOPE][None, :], (n, A_NOPE))
    g_rope = gain[A_NOPE:]
    zeros32 = jnp.zeros((n, A_ROPE), F32)
    tab1 = jnp.concatenate([g_nope, g_rope[None, :] * cos, zeros32], axis=-1)
    tab2 = jnp.concatenate([jnp.zeros((n, A_NOPE + A_ROPE), F32), g_rope[src][None, :] * sin], axis=-1)
    return tab1 * extra_scale, tab2 * extra_scale


def _layer_weights(l, p, src, sign):
    w_in = p["w_in"][l]
    d = w_in.shape[0]
    sizes = (F_WIDTH, F_WIDTH, G_WIDTH, G_WIDTH, G_WIDTH, Q_LORA, KV_LORA, A_ROPE, A_WIDTH)
    parts, start = [], 0
    for s in sizes:
        parts.append(w_in[:, start:start + s])
        start += s
    w_kr = parts[7]
    kr_tile = jnp.concatenate([jnp.zeros((d, A_NOPE), F32), w_kr, w_kr[:, src] * sign[None, :]], axis=-1)
    w_in_p = jnp.concatenate(parts[:7] + [kr_tile, parts[8]], axis=-1).astype(BF16)

    w_uq = p["w_uq"][l].reshape(Q_LORA, A_HEADS, A_QK)
    w_uq_rope = w_uq[:, :, A_NOPE:]
    w_uq_p = jnp.concatenate([w_uq, w_uq_rope[:, :, src] * sign[None, None, :]], axis=-1)
    w_uq_p = w_uq_p.reshape(Q_LORA, A_HEADS * LANES).astype(BF16)

    w_ukv = p["w_ukv"][l].reshape(KV_LORA, A_HEADS, A_NOPE + A_V)
    w_uk_p = jnp.concatenate([w_ukv[:, :, :A_NOPE], jnp.zeros((KV_LORA, A_HEADS, LANES - A_NOPE), F32)], axis=-1)
    w_uk_p = w_uk_p.reshape(KV_LORA, A_HEADS * LANES).astype(BF16)
    w_uv = w_ukv[:, :, A_NOPE:].reshape(KV_LORA, A_WIDTH).astype(BF16)

    ws = p["g_ws"][l]
    ws_cat = jnp.transpose(ws, (1, 0, 2)).reshape(CHUNK, G_HEADS * CHUNK).astype(BF16)
    bs_tab = jnp.repeat(p["g_bs"][l].T, G_HEAD_DIM, axis=1)
    return {
        "norm_g": p["norm_g"][l][None, :], "w_in": w_in_p, "w_uq": w_uq_p, "w_uk": w_uk_p, "w_uv": w_uv,
        "g_ln_g": jnp.tile(p["g_ln_g"][l], G_HEADS)[None, :], "ws_cat": ws_cat, "bs_tab": bs_tab,
        "q_a_g": p["q_a_g"][l][None, :], "kv_a_g": p["kv_a_g"][l][None, :],
        "w_fmix": p["w_fmix"][l].astype(BF16), "w_out": p["w_out"][l].astype(BF16),
    }


def _token_tile(n):
    return 512 if n % 512 == 0 else 256


def kernel(x, c, ctx, c_ctx, w_mod, b_mod, norm_g, w_in, w_fmix, g_ln_g, g_ws, g_bs,
           q_a_g, w_uq, kv_a_g, w_ukv, q_norm_g, k_norm_g, w_out):
    p = dict(w_in=w_in, w_fmix=w_fmix, g_ln_g=g_ln_g, g_ws=g_ws, g_bs=g_bs, q_a_g=q_a_g, w_uq=w_uq,
             kv_a_g=kv_a_g, w_ukv=w_ukv, norm_g=norm_g, w_out=w_out)
    depth = w_mod.shape[0]
    b, n, d = x.shape
    n_ctx = ctx.shape[1]
    src, sign = _rot_perm_sign()

    rows = -(-(b + 1) // 8) * 8
    cc = jnp.concatenate([c, c_ctx[None, :], jnp.zeros((rows - b - 1, d), F32)], axis=0)
    mod = _modulation(_silu(cc), w_mod, b_mod)

    dft = {m: _dft_tables(m) for m in {n, n_ctx}}
    dft = {m: (cs.astype(BF16), (-sn).astype(BF16)) for m, (cs, sn) in dft.items()}
    cos64, sin64 = _dft_tables(F_GROUP_DIM)
    eye_g = jnp.eye(F_GROUPS, dtype=F32)
    bc = jnp.kron(eye_g, cos64).astype(BF16)
    bs = jnp.kron(eye_g, sin64).astype(BF16)
    jm = jnp.kron(jnp.eye(G_HEADS, dtype=F32), jnp.full((G_HEAD_DIM, G_HEAD_DIM), 1.0 / G_HEAD_DIM, F32)).astype(BF16)
    cos_x, sin_x = _rope_cos_sin(n)
    cos_y, sin_y = jnp.ones((n_ctx, A_ROPE), F32), jnp.zeros((n_ctx, A_ROPE), F32)
    q_scale = (A_QK ** -0.5) * math.log2(math.e)

    y = ctx
    for l in range(depth):
        lw = _layer_weights(l, p, src, sign)
        lw.update(bc=bc, bs=bs, jm=jm)
        shift, scale, gate = (mod[l, :, i * d:(i + 1) * d] for i in range(3))
        tabs_x, tabs_y = {}, {}
        tabs_x["q1"], tabs_x["q2"] = _head_tables(q_norm_g[l], cos_x, sin_x, src, q_scale)
        tabs_x["k1"], tabs_x["k2"] = _head_tables(k_norm_g[l], cos_x, sin_x, src, 1.0)
        tabs_y["q1"], tabs_y["q2"] = _head_tables(q_norm_g[l], cos_y, sin_y, src, q_scale)
        tabs_y["k1"], tabs_y["k2"] = _head_tables(k_norm_g[l], cos_y, sin_y, src, 1.0)

        tm_y = _token_tile(n_ctx)
        fa_y, fb_y, sfg_y, gout_y, q_y, k_y, vt_y, sag_y = _in_proj(
            y, scale[b:b + 1, None, :], shift[b:b + 1, None, :], lw, tabs_y, tm_y)
        tm_x = _token_tile(n)
        fa_x, fb_x, sfg_x, gout_x, q_x, k_x, vt_x, sag_x = _in_proj(
            x, scale[:b, None, :], shift[:b, None, :], lw, tabs_x, tm_x)

        f_x = _pos_dft(*dft[n], fa_x, fb_x, tm_x)
        o_x = _attention(q_x, [k_y, k_x], [vt_y, vt_x], 256)
        x = _out_proj(x, gate[:b, None, :], f_x, sfg_x, gout_x, o_x, sag_x, lw["w_fmix"], lw["w_out"], tm_x)
        if l < depth - 1:
            f_y = _pos_dft(*dft[n_ctx], fa_y, fb_y, tm_y)
            o_y = _attention(q_y, [k_y], [vt_y], 256)
            y = _out_proj(y, gate[b:b + 1, None, :], f_y, sfg_y, gout_y, o_y, sag_y,
                          lw["w_fmix"], lw["w_out"], tm_y)
    return x
```

```python
import functools
import math

import jax
import jax.numpy as jnp
from jax import lax
from jax.experimental import pallas as pl
from jax.experimental.pallas import tpu as pltpu

F32 = jnp.float32
BF16 = jnp.bfloat16

LANES = 128
VMEM_LIMIT_BYTES = 56 * 1024 * 1024

GRID_W = 64
F_GROUPS = 4
F_GROUP_DIM = 64
F_WIDTH = F_GROUPS * F_GROUP_DIM
G_HEADS = 4
G_HEAD_DIM = 64
G_WIDTH = G_HEADS * G_HEAD_DIM
CHUNK = 128
A_HEADS = 8
A_NOPE = 64
A_ROPE = 32
A_V = 64
A_QK = A_NOPE + A_ROPE
A_WIDTH = A_HEADS * A_V
Q_LORA = 256
KV_LORA = 128
ROPE_BASE = 10000.0
EPS = 1e-6
ONES_ROWS = 16
KEY_CHUNK = 256

OFF_FIN = 0
OFF_FGATE = OFF_FIN + F_WIDTH
OFF_GU = OFF_FGATE + F_WIDTH
OFF_GV = OFF_GU + G_WIDTH
OFF_GGATE = OFF_GV + G_WIDTH
OFF_CQ = OFF_GGATE + G_WIDTH
OFF_CKV = OFF_CQ + Q_LORA
OFF_KR = OFF_CKV + KV_LORA
OFF_KR2 = OFF_KR + LANES
OFF_AGATE = OFF_KR2 + LANES
IN_P = OFF_AGATE + A_WIDTH


def _silu(t):
    return t * (1.0 / (1.0 + jnp.exp(-t)))


def _dot(a, b):
    return jnp.dot(a, b, preferred_element_type=F32)


def _dot_nt(a, b):
    return lax.dot_general(a, b, (((1,), (1,)), ((), ())), preferred_element_type=F32)


def _dot_hilo(t, w):
    hi = t.astype(BF16)
    lo = (t - hi.astype(F32)).astype(BF16)
    return _dot(hi, w) + _dot(lo, w)


def _mod_kernel(cc_ref, w_ref, b_ref, o_ref):
    o_ref[0] = lax.dot_general(cc_ref[...], w_ref[0], (((1,), (0,)), ((), ())),
                               precision=lax.Precision.HIGHEST,
                               preferred_element_type=F32) + b_ref[0]


def _modulation(cc, w_mod, b_mod):
    depth, d, d3 = w_mod.shape
    rows = cc.shape[0]
    tn = 1024
    return pl.pallas_call(
        _mod_kernel,
        grid=(depth, d3 // tn),
        in_specs=[pl.BlockSpec((rows, d), lambda l, j: (0, 0)),
                  pl.BlockSpec((1, d, tn), lambda l, j: (l, 0, j)),
                  pl.BlockSpec((1, 1, tn), lambda l, j: (l, 0, j))],
        out_specs=pl.BlockSpec((1, rows, tn), lambda l, j: (l, 0, j)),
        out_shape=jax.ShapeDtypeStruct((depth, rows, d3), F32),
        compiler_params=pltpu.CompilerParams(vmem_limit_bytes=VMEM_LIMIT_BYTES),
        name="modulation",
    )(cc, w_mod, b_mod.reshape(depth, 1, d3))


def _in_proj_kernel(x_ref, sc_ref, sh_ref, ng_ref, win_ref, bc_ref, bs_ref, jm_ref, gln_ref,
                    ws_ref, bsb_ref, qag_ref, wuqt_ref, kvag_ref, wuk_ref, wuvt_ref,
                    tqt_ref, tk1_ref, tk2_ref, *rest):
    fa_ref, fb_ref, sfg_ref, gout_ref, q_ref, k_ref, vt_ref, sag_ref = rest[-8:]
    tm = x_ref.shape[1]
    x = x_ref[0]
    xn = x * lax.rsqrt(jnp.mean(x * x, axis=-1, keepdims=True) + EPS) * ng_ref[...]
    hb = (xn * (1.0 + sc_ref[0]) + sh_ref[0]).astype(BF16)

    f_in = _dot(hb, win_ref[:, OFF_FIN:OFF_FIN + F_WIDTH]).astype(BF16)
    fa_ref[0] = _dot(f_in, bc_ref[...]).astype(BF16)
    fb_ref[0] = _dot(f_in, bs_ref[...]).astype(BF16)
    sfg_ref[0] = _silu(_dot(hb, win_ref[:, OFF_FGATE:OFF_FGATE + F_WIDTH]))

    g_u = _dot(hb, win_ref[:, OFF_GU:OFF_GU + G_WIDTH])
    g_v = _dot(hb, win_ref[:, OFF_GV:OFF_GV + G_WIDTH])
    g_gate = _dot(hb, win_ref[:, OFF_GGATE:OFF_GGATE + G_WIDTH])
    jm = jm_ref[...]
    dv = g_v - _dot_hilo(g_v, jm)
    vln = (dv * lax.rsqrt(_dot_hilo(dv * dv, jm) + EPS) * gln_ref[...])
    head_of_lane = lax.broadcasted_iota(jnp.int32, (CHUNK, G_WIDTH), 1) // G_HEAD_DIM
    for c in range(tm // CHUNK):
        rows = slice(c * CHUNK, (c + 1) * CHUNK)
        vc = vln[rows]
        stacked = jnp.concatenate(
            [jnp.where(head_of_lane == h, vc, 0.0).astype(BF16) for h in range(G_HEADS)], axis=0)
        mixed = _dot(ws_ref[...], stacked) + bsb_ref[...]
        gout_ref[0, rows, :] = (g_u[rows] * mixed * _silu(g_gate[rows])).astype(BF16)

    c_q = _dot(hb, win_ref[:, OFF_CQ:OFF_CQ + Q_LORA])
    cqn = c_q * lax.rsqrt(jnp.mean(c_q * c_q, axis=-1, keepdims=True) + EPS) * qag_ref[...]
    qt_all = _dot(wuqt_ref[...], cqn.T.astype(BF16))
    tqt = tqt_ref[...]
    for h in range(A_HEADS):
        t = qt_all[h * LANES:(h + 1) * LANES]
        ss = jnp.sum(t[0:A_QK] * t[0:A_QK], axis=0, keepdims=True)
        q_ref[0, h] = (t * lax.rsqrt(ss * (1.0 / A_QK) + EPS) * tqt).astype(BF16)

    lane = lax.broadcasted_iota(jnp.int32, (1, LANES), 1)
    lane_mask = (lane < A_QK).astype(F32)
    c_kv = _dot(hb, win_ref[:, OFF_CKV:OFF_CKV + KV_LORA])
    ckvn = c_kv * lax.rsqrt(jnp.mean(c_kv * c_kv, axis=-1, keepdims=True) + EPS) * kvag_ref[...]
    ckvn_b = ckvn.astype(BF16)
    kr = _dot(hb, win_ref[:, OFF_KR:OFF_KR + LANES])
    kr2 = _dot(hb, win_ref[:, OFF_KR2:OFF_KR2 + LANES]) * tk2_ref[...]
    tk1 = tk1_ref[...]
    for h in range(A_HEADS):
        t = _dot(ckvn_b, wuk_ref[:, h * LANES:(h + 1) * LANES]) + kr
        ss = jnp.sum(t * t * lane_mask, axis=-1, keepdims=True)
        k_ref[0, h] = ((t * tk1 + kr2) * lax.rsqrt(ss * (1.0 / A_QK) + EPS)).astype(BF16)
    vt = _dot(wuvt_ref[...], ckvn.T.astype(BF16))
    vt_ref[0, :, 0:A_V, :] = vt.astype(BF16).reshape(A_HEADS, A_V, tm)
    vt_ref[0, :, A_V:, :] = jnp.ones((A_HEADS, ONES_ROWS, tm), BF16)

    sag_ref[0] = _silu(_dot(hb, win_ref[:, OFF_AGATE:OFF_AGATE + A_WIDTH]))


def _in_proj(x, scale, shift, lw, tabs, tm, n_keys, key_start, kv_bufs=None):
    b, n, d = x.shape
    key_blk = key_start // tm
    mod_map = (lambda bi, i: (bi, 0, 0)) if scale.shape[0] > 1 else (lambda bi, i: (0, 0, 0))
    const2 = lambda bi, i: (0, 0)
    row_tab = lambda bi, i: (i, 0)
    tok = lambda bi, i: (bi, i, 0)
    full = lambda a: pl.BlockSpec(a.shape, const2)
    in_specs = [
        pl.BlockSpec((1, tm, d), tok),
        pl.BlockSpec((1, 1, d), mod_map),
        pl.BlockSpec((1, 1, d), mod_map),
        full(lw["norm_g"]), full(lw["w_in"]), full(lw["bc"]), full(lw["bs"]), full(lw["jm"]),
        full(lw["g_ln_g"]), full(lw["ws_cat"]), full(lw["bs_tab"]), full(lw["q_a_g"]),
        full(lw["w_uq_t"]), full(lw["kv_a_g"]), full(lw["w_uk"]), full(lw["w_uv_t"]),
        pl.BlockSpec((LANES, tm), lambda bi, i: (0, i)),
        pl.BlockSpec((tm, LANES), row_tab), pl.BlockSpec((tm, LANES), row_tab),
    ]
    out_shape = [
        jax.ShapeDtypeStruct((b, n, F_WIDTH), BF16),
        jax.ShapeDtypeStruct((b, n, F_WIDTH), BF16),
        jax.ShapeDtypeStruct((b, n, F_WIDTH), F32),
        jax.ShapeDtypeStruct((b, n, G_WIDTH), BF16),
        jax.ShapeDtypeStruct((b, A_HEADS, LANES, n), BF16),
        jax.ShapeDtypeStruct((b, A_HEADS, n_keys, LANES), BF16),
        jax.ShapeDtypeStruct((b, A_HEADS, A_V + ONES_ROWS, n_keys), BF16),
        jax.ShapeDtypeStruct((b, n, A_WIDTH), F32),
    ]
    out_specs = [
        pl.BlockSpec((1, tm, F_WIDTH), tok),
        pl.BlockSpec((1, tm, F_WIDTH), tok),
        pl.BlockSpec((1, tm, F_WIDTH), tok),
        pl.BlockSpec((1, tm, G_WIDTH), tok),
        pl.BlockSpec((1, A_HEADS, LANES, tm), lambda bi, i: (bi, 0, 0, i)),
        pl.BlockSpec((1, A_HEADS, tm, LANES), lambda bi, i: (bi, 0, key_blk + i, 0)),
        pl.BlockSpec((1, A_HEADS, A_V + ONES_ROWS, tm), lambda bi, i: (bi, 0, 0, key_blk + i)),
        pl.BlockSpec((1, tm, A_WIDTH), tok),
    ]
    args = [x, scale, shift, lw["norm_g"], lw["w_in"], lw["bc"], lw["bs"], lw["jm"], lw["g_ln_g"],
            lw["ws_cat"], lw["bs_tab"], lw["q_a_g"], lw["w_uq_t"], lw["kv_a_g"], lw["w_uk"], lw["w_uv_t"],
            tabs["qt"], tabs["k1"], tabs["k2"]]
    aliases = {}
    if kv_bufs is not None:
        aliases = {len(args): 5, len(args) + 1: 6}
        in_specs += [pl.BlockSpec(memory_space=pl.ANY), pl.BlockSpec(memory_space=pl.ANY)]
        args += list(kv_bufs)
    return pl.pallas_call(
        _in_proj_kernel,
        grid=(b, n // tm),
        in_specs=in_specs,
        out_specs=out_specs,
        out_shape=out_shape,
        input_output_aliases=aliases,
        compiler_params=pltpu.CompilerParams(
            dimension_semantics=("arbitrary", "arbitrary"), vmem_limit_bytes=VMEM_LIMIT_BYTES),
        name="in_proj",
    )(*args)


def _pos_dft_kernel(cn_ref, sn_ref, fa_ref, fb_ref, o_ref, *, norm):
    o_ref[0] = (_dot(cn_ref[...], fa_ref[0]) + _dot(sn_ref[...], fb_ref[0])) * norm


def _pos_dft(cos_n, nsin_n, fa, fb, tn):
    b, n, w = fa.shape
    return pl.pallas_call(
        functools.partial(_pos_dft_kernel, norm=1.0 / math.sqrt(n * F_GROUP_DIM)),
        grid=(n // tn, b),
        in_specs=[pl.BlockSpec((tn, n), lambda i, bi: (i, 0)),
                  pl.BlockSpec((tn, n), lambda i, bi: (i, 0)),
                  pl.BlockSpec((1, n, w), lambda i, bi: (bi, 0, 0)),
                  pl.BlockSpec((1, n, w), lambda i, bi: (bi, 0, 0))],
        out_specs=pl.BlockSpec((1, tn, w), lambda i, bi: (bi, i, 0)),
        out_shape=jax.ShapeDtypeStruct((b, n, w), F32),
        compiler_params=pltpu.CompilerParams(
            dimension_semantics=("arbitrary", "arbitrary"), vmem_limit_bytes=VMEM_LIMIT_BYTES),
        name="pos_dft",
    )(cos_n, nsin_n, fa, fb)


def _attention_kernel(qt_ref, k_ref, vt_ref, o_ref, s0_ref, s1_ref, ot_ref):
    heads, tq = qt_ref.shape[1], qt_ref.shape[3]
    nchunks = k_ref.shape[2] // KEY_CHUNK
    s_bufs = (s0_ref, s1_ref)

    def phase(h_prod, h_cons, m_cons, prod_buf):
        m_part, acc = None, None
        for c in range(nchunks):
            rows = slice(c * KEY_CHUNK, (c + 1) * KEY_CHUNK)
            if h_prod is not None:
                s_t = _dot(k_ref[0, h_prod, rows, :], qt_ref[0, h_prod])
                s_bufs[prod_buf][rows, :] = s_t
                pm = jnp.max(s_t.reshape(KEY_CHUNK // 8, 8, tq), axis=0)
                m_part = pm if m_part is None else jnp.maximum(m_part, pm)
            if h_cons is not None:
                p_t = jnp.exp2(s_bufs[1 - prod_buf][rows, :] - m_cons).astype(BF16)
                d = _dot(vt_ref[0, h_cons, :, rows], p_t)
                acc = d if acc is None else acc + d
        if h_cons is not None:
            ot_ref[pl.ds(pl.multiple_of(h_cons * A_V, A_V), A_V), :] = acc[0:A_V] / acc[A_V:A_V + 1]
        return None if h_prod is None else jnp.max(m_part, axis=0, keepdims=True)

    def head_pair(j, m):
        m = phase(2 * j + 1, 2 * j, m, 1)
        return phase(2 * j + 2, 2 * j + 1, m, 0)

    m = phase(0, None, None, 0)
    m = lax.fori_loop(0, (heads - 2) // 2, head_pair, m)
    m = phase(heads - 1, heads - 2, m, 1)
    phase(None, heads - 1, m, 0)
    o_ref[0] = ot_ref[...].T


def _attention(qt, k_all, vt_all, key_start, n_keys, tq):
    b, heads, _, n = qt.shape
    key_blk = key_start // n_keys
    resident = pl.Buffered(1)
    return pl.pallas_call(
        _attention_kernel,
        grid=(b, n // tq),
        in_specs=[pl.BlockSpec((1, heads, LANES, tq), lambda bi, qi: (bi, 0, 0, qi)),
                  pl.BlockSpec((1, heads, n_keys, LANES), lambda bi, qi: (bi, 0, key_blk, 0),
                               pipeline_mode=resident),
                  pl.BlockSpec((1, heads, A_V + ONES_ROWS, n_keys), lambda bi, qi: (bi, 0, 0, key_blk),
                               pipeline_mode=resident)],
        out_specs=pl.BlockSpec((1, tq, heads * A_V), lambda bi, qi: (bi, qi, 0)),
        out_shape=jax.ShapeDtypeStruct((b, n, heads * A_V), F32),
        scratch_shapes=[pltpu.VMEM((n_keys, tq), F32), pltpu.VMEM((n_keys, tq), F32),
                        pltpu.VMEM((heads * A_V, tq), F32)],
        compiler_params=pltpu.CompilerParams(
            dimension_semantics=("arbitrary", "arbitrary"),
            vmem_limit_bytes=VMEM_LIMIT_BYTES),
        name="attention",
    )(qt, k_all, vt_all)


def _out_proj_kernel(x_ref, gate_ref, f_ref, sfg_ref, gout_ref, o_ref, sag_ref, wf_ref, wo_ref, y_ref):
    f_out = (_dot(f_ref[0].astype(BF16), wf_ref[...]) * sfg_ref[0]).astype(BF16)
    o_g = (o_ref[0] * sag_ref[0]).astype(BF16)
    y = (_dot(f_out, wo_ref[0:F_WIDTH, :])
         + _dot(gout_ref[0], wo_ref[F_WIDTH:F_WIDTH + G_WIDTH, :])
         + _dot(o_g, wo_ref[F_WIDTH + G_WIDTH:, :]))
    y_ref[0] = x_ref[0] + gate_ref[0] * y


def _out_proj(x, gate, f, sfg, gout, o, sag, w_fmix, w_out, tm):
    b, n, d = x.shape
    tok = lambda bi, i: (bi, i, 0)
    mod_map = (lambda bi, i: (bi, 0, 0)) if gate.shape[0] > 1 else (lambda bi, i: (0, 0, 0))
    const2 = lambda bi, i: (0, 0)
    return pl.pallas_call(
        _out_proj_kernel,
        grid=(b, n // tm),
        in_specs=[pl.BlockSpec((1, tm, d), tok),
                  pl.BlockSpec((1, 1, d), mod_map),
                  pl.BlockSpec((1, tm, F_WIDTH), tok),
                  pl.BlockSpec((1, tm, F_WIDTH), tok),
                  pl.BlockSpec((1, tm, G_WIDTH), tok),
                  pl.BlockSpec((1, tm, A_WIDTH), tok),
                  pl.BlockSpec((1, tm, A_WIDTH), tok),
                  pl.BlockSpec(w_fmix.shape, const2),
                  pl.BlockSpec(w_out.shape, const2)],
        out_specs=pl.BlockSpec((1, tm, d), tok),
        out_shape=jax.ShapeDtypeStruct((b, n, d), F32),
        compiler_params=pltpu.CompilerParams(
            dimension_semantics=("arbitrary", "arbitrary"), vmem_limit_bytes=VMEM_LIMIT_BYTES),
        name="out_proj",
    )(x, gate, f, sfg, gout, o, sag, w_fmix, w_out)


def _rot_perm_sign():
    q = A_ROPE // 4
    src, sign = [], []
    for j in range(A_ROPE):
        blk, r = divmod(j, q)
        if blk % 2 == 0:
            src.append((blk + 1) * q + r)
            sign.append(-1.0)
        else:
            src.append((blk - 1) * q + r)
            sign.append(1.0)
    return jnp.array(src, jnp.int32), jnp.array(sign, F32)


def _dft_tables(n):
    idx = (jnp.arange(n, dtype=jnp.int32)[:, None] * jnp.arange(n, dtype=jnp.int32)[None, :]) % n
    ang = idx.astype(F32) * (2.0 * math.pi / n)
    return jnp.cos(ang), jnp.sin(ang)


def _rope_cos_sin(n):
    rows = n // GRID_W
    row = jnp.repeat(jnp.arange(rows, dtype=F32), GRID_W)
    col = jnp.tile(jnp.arange(GRID_W, dtype=F32), rows)
    half = A_ROPE // 2
    inv = ROPE_BASE ** (-jnp.arange(0, half, 2, dtype=F32) / half)
    ang_r = row[:, None] * inv[None, :]
    ang_c = col[:, None] * inv[None, :]
    ang = jnp.concatenate([ang_r, ang_r, ang_c, ang_c], axis=-1)
    return jnp.cos(ang), jnp.sin(ang)


def _head_tables(gain, cos, sin, src):
    n = cos.shape[0]
    g_nope = jnp.broadcast_to(gain[:A_NOPE][None, :], (n, A_NOPE))
    a = gain[A_NOPE:][None, :] * cos
    b = gain[A_NOPE:][src][None, :] * sin
    tab1 = jnp.concatenate([g_nope, a, b], axis=-1)
    tab2 = jnp.concatenate([jnp.zeros((n, A_NOPE), F32), b, a], axis=-1)
    return tab1, tab2


def _layer_weights(l, p, src, sign):
    w_in = p["w_in"][l]
    d = w_in.shape[0]
    sizes = (F_WIDTH, F_WIDTH, G_WIDTH, G_WIDTH, G_WIDTH, Q_LORA, KV_LORA, A_ROPE, A_WIDTH)
    parts, start = [], 0
    for s in sizes:
        parts.append(w_in[:, start:start + s])
        start += s
    w_kr = parts[7]
    w_rot = w_kr[:, src] * sign[None, :]
    kr_tile = jnp.concatenate([jnp.zeros((d, A_NOPE), F32), w_kr, w_rot], axis=-1)
    kr2_tile = jnp.concatenate([jnp.zeros((d, A_NOPE), F32), w_rot, w_kr], axis=-1)
    w_in_p = jnp.concatenate(parts[:7] + [kr_tile, kr2_tile, parts[8]], axis=-1).astype(BF16)

    w_uq = p["w_uq"][l].reshape(Q_LORA, A_HEADS, A_QK)
    w_uq_rope = w_uq[:, :, A_NOPE:]
    w_uq_p = jnp.concatenate([w_uq, w_uq_rope[:, :, src] * sign[None, None, :]], axis=-1)
    w_uq_t = w_uq_p.reshape(Q_LORA, A_HEADS * LANES).T.astype(BF16)

    w_ukv = p["w_ukv"][l].reshape(KV_LORA, A_HEADS, A_NOPE + A_V)
    w_uk_p = jnp.concatenate([w_ukv[:, :, :A_NOPE], jnp.zeros((KV_LORA, A_HEADS, LANES - A_NOPE), F32)], axis=-1)
    w_uk_p = w_uk_p.reshape(KV_LORA, A_HEADS * LANES).astype(BF16)
    w_uv_t = w_ukv[:, :, A_NOPE:].reshape(KV_LORA, A_WIDTH).T.astype(BF16)

    ws = p["g_ws"][l]
    ws_cat = jnp.transpose(ws, (1, 0, 2)).reshape(CHUNK, G_HEADS * CHUNK).astype(BF16)
    bs_tab = jnp.repeat(p["g_bs"][l].T, G_HEAD_DIM, axis=1)
    return {
        "norm_g": p["norm_g"][l][None, :], "w_in": w_in_p, "w_uq_t": w_uq_t, "w_uk": w_uk_p, "w_uv_t": w_uv_t,
        "g_ln_g": jnp.tile(p["g_ln_g"][l], G_HEADS)[None, :], "ws_cat": ws_cat, "bs_tab": bs_tab,
        "q_a_g": p["q_a_g"][l][None, :], "kv_a_g": p["kv_a_g"][l][None, :],
        "w_fmix": p["w_fmix"][l].astype(BF16), "w_out": p["w_out"][l].astype(BF16),
    }


def _token_tile(n):
    return 512 if n % 512 == 0 else 256


def kernel(x, c, ctx, c_ctx, w_mod, b_mod, norm_g, w_in, w_fmix, g_ln_g, g_ws, g_bs,
           q_a_g, w_uq, kv_a_g, w_ukv, q_norm_g, k_norm_g, w_out):
    p = dict(w_in=w_in, w_fmix=w_fmix, g_ln_g=g_ln_g, g_ws=g_ws, g_bs=g_bs, q_a_g=q_a_g, w_uq=w_uq,
             kv_a_g=kv_a_g, w_ukv=w_ukv, norm_g=norm_g, w_out=w_out)
    depth = w_mod.shape[0]
    b, n, d = x.shape
    n_ctx = ctx.shape[1]
    src, sign = _rot_perm_sign()

    rows = -(-(b + 1) // 8) * 8
    cc = jnp.concatenate([c, c_ctx[None, :], jnp.zeros((rows - b - 1, d), F32)], axis=0)
    mod = _modulation(_silu(cc), w_mod, b_mod)

    dft = {m: _dft_tables(m) for m in {n, n_ctx}}
    dft = {m: (cs.astype(BF16), (-sn).astype(BF16)) for m, (cs, sn) in dft.items()}
    cos64, sin64 = _dft_tables(F_GROUP_DIM)
    eye_g = jnp.eye(F_GROUPS, dtype=F32)
    bc = jnp.kron(eye_g, cos64).astype(BF16)
    bs = jnp.kron(eye_g, sin64).astype(BF16)
    jm = jnp.kron(jnp.eye(G_HEADS, dtype=F32), jnp.full((G_HEAD_DIM, G_HEAD_DIM), 1.0 / G_HEAD_DIM, F32)).astype(BF16)
    cos_x, sin_x = _rope_cos_sin(n)
    cos_y, sin_y = jnp.ones((n_ctx, A_ROPE), F32), jnp.zeros((n_ctx, A_ROPE), F32)
    q_scale = (A_QK ** -0.5) * math.log2(math.e)

    y = ctx
    for l in range(depth):
        lw = _layer_weights(l, p, src, sign)
        lw.update(bc=bc, bs=bs, jm=jm)
        shift, scale, gate = (mod[l, :, i * d:(i + 1) * d] for i in range(3))
        tabs_x, tabs_y = {}, {}
        for tabs, cs, sn in ((tabs_x, cos_x, sin_x), (tabs_y, cos_y, sin_y)):
            tabs["qt"] = (_head_tables(q_norm_g[l], cs, sn, src)[0] * q_scale).T
            tabs["k1"], tabs["k2"] = _head_tables(k_norm_g[l], cs, sn, src)

        tm_y, tm_x = _token_tile(n_ctx), _token_tile(n)
        n_keys = n + n_ctx
        fa_y, fb_y, sfg_y, gout_y, q_y, k_all, vt_all, sag_y = _in_proj(
            y, scale[b:b + 1, None, :], shift[b:b + 1, None, :], lw, tabs_y, tm_y, n_keys, n)
        fa_x, fb_x, sfg_x, gout_x, q_x, k_all, vt_all, sag_x = _in_proj(
            x, scale[:b, None, :], shift[:b, None, :], lw, tabs_x, tm_x, n_keys, 0, (k_all, vt_all))

        f_x = _pos_dft(*dft[n], fa_x, fb_x, tm_x)
        o_x = _attention(q_x, k_all, vt_all, 0, n_keys, tm_x)
        x = _out_proj(x, gate[:b, None, :], f_x, sfg_x, gout_x, o_x, sag_x, lw["w_fmix"], lw["w_out"], tm_x)
        if l < depth - 1:
            f_y = _pos_dft(*dft[n_ctx], fa_y, fb_y, tm_y)
            o_y = _attention(q_y, k_all, vt_all, n, n_ctx, tm_y)
            y = _out_proj(y, gate[b:b + 1, None, :], f_y, sfg_y, gout_y, o_y, sag_y,
                          lw["w_fmix"], lw["w_out"], tm_y)
    return x
```

```python
import functools
import math

import jax
import jax.numpy as jnp
from jax import lax
from jax.experimental import pallas as pl
from jax.experimental.pallas import tpu as pltpu

F32 = jnp.float32
BF16 = jnp.bfloat16

LANES = 128
VMEM_LIMIT_BYTES = 56 * 1024 * 1024

GRID_W = 64
F_GROUPS = 4
F_GROUP_DIM = 64
F_WIDTH = F_GROUPS * F_GROUP_DIM
G_HEADS = 4
G_HEAD_DIM = 64
G_WIDTH = G_HEADS * G_HEAD_DIM
CHUNK = 128
A_HEADS = 8
A_NOPE = 64
A_ROPE = 32
A_V = 64
A_QK = A_NOPE + A_ROPE
A_WIDTH = A_HEADS * A_V
Q_LORA = 256
KV_LORA = 128
ROPE_BASE = 10000.0
EPS = 1e-6
ONES_ROWS = 16
KEY_CHUNK = 256

OFF_FIN = 0
OFF_FGATE = OFF_FIN + F_WIDTH
OFF_GU = OFF_FGATE + F_WIDTH
OFF_GV = OFF_GU + G_WIDTH
OFF_GGATE = OFF_GV + G_WIDTH
OFF_CQ = OFF_GGATE + G_WIDTH
OFF_CKV = OFF_CQ + Q_LORA
OFF_KR = OFF_CKV + KV_LORA
OFF_KR2 = OFF_KR + LANES
OFF_AGATE = OFF_KR2 + LANES
IN_P = OFF_AGATE + A_WIDTH


def _silu(t):
    return t * (1.0 / (1.0 + jnp.exp(-t)))


def _dot(a, b):
    return jnp.dot(a, b, preferred_element_type=F32)


def _dot_hilo(t, w):
    hi = t.astype(BF16)
    lo = (t - hi.astype(F32)).astype(BF16)
    return _dot(hi, w) + _dot(lo, w)


def _mod_kernel(cc_ref, w_ref, b_ref, o_ref):
    o_ref[0] = lax.dot_general(cc_ref[...], w_ref[0], (((1,), (0,)), ((), ())),
                               precision=lax.Precision.HIGHEST,
                               preferred_element_type=F32) + b_ref[0]


def _modulation(cc, w_mod, b_mod):
    depth, d, d3 = w_mod.shape
    rows = cc.shape[0]
    tn = 1024
    return pl.pallas_call(
        _mod_kernel,
        grid=(depth, d3 // tn),
        in_specs=[pl.BlockSpec((rows, d), lambda l, j: (0, 0)),
                  pl.BlockSpec((1, d, tn), lambda l, j: (l, 0, j)),
                  pl.BlockSpec((1, 1, tn), lambda l, j: (l, 0, j))],
        out_specs=pl.BlockSpec((1, rows, tn), lambda l, j: (l, 0, j)),
        out_shape=jax.ShapeDtypeStruct((depth, rows, d3), F32),
        compiler_params=pltpu.CompilerParams(vmem_limit_bytes=VMEM_LIMIT_BYTES),
        name="modulation",
    )(cc, w_mod, b_mod.reshape(depth, 1, d3))


def _in_proj_kernel(x_ref, sc_ref, sh_ref, ng_ref, win_ref, jm_ref, gln_ref,
                    ws_ref, bsb_ref, qag_ref, wuqt_ref, kvag_ref, wuk_ref, wuvt_ref,
                    tqt_ref, tk1_ref, tk2_ref, *rest):
    fin_ref, sfg_ref, gout_ref, q_ref, k_ref, vt_ref, sag_ref = rest[-7:]
    tm = x_ref.shape[1]
    x = x_ref[0]
    xn = x * lax.rsqrt(jnp.mean(x * x, axis=-1, keepdims=True) + EPS) * ng_ref[...]
    hb = (xn * (1.0 + sc_ref[0]) + sh_ref[0]).astype(BF16)

    fin_ref[0] = _dot(hb, win_ref[:, OFF_FIN:OFF_FIN + F_WIDTH]).astype(BF16)
    sfg_ref[0] = _silu(_dot(hb, win_ref[:, OFF_FGATE:OFF_FGATE + F_WIDTH])).astype(BF16)

    g_u = _dot(hb, win_ref[:, OFF_GU:OFF_GU + G_WIDTH])
    g_v = _dot(hb, win_ref[:, OFF_GV:OFF_GV + G_WIDTH])
    g_gate = _dot(hb, win_ref[:, OFF_GGATE:OFF_GGATE + G_WIDTH])
    jm = jm_ref[...]
    dv = g_v - _dot_hilo(g_v, jm)
    vln = (dv * lax.rsqrt(_dot_hilo(dv * dv, jm) + EPS) * gln_ref[...])
    head_of_lane = lax.broadcasted_iota(jnp.int32, (CHUNK, G_WIDTH), 1) // G_HEAD_DIM
    for c in range(tm // CHUNK):
        rows = slice(c * CHUNK, (c + 1) * CHUNK)
        vc = vln[rows]
        stacked = jnp.concatenate(
            [jnp.where(head_of_lane == h, vc, 0.0).astype(BF16) for h in range(G_HEADS)], axis=0)
        mixed = _dot(ws_ref[...], stacked) + bsb_ref[...]
        gout_ref[0, rows, :] = (g_u[rows] * mixed * _silu(g_gate[rows])).astype(BF16)

    c_q = _dot(hb, win_ref[:, OFF_CQ:OFF_CQ + Q_LORA])
    cqn = c_q * lax.rsqrt(jnp.mean(c_q * c_q, axis=-1, keepdims=True) + EPS) * qag_ref[...]
    qt_all = _dot(wuqt_ref[...], cqn.T.astype(BF16))
    tqt = tqt_ref[...]
    for h in range(A_HEADS):
        t = qt_all[h * LANES:(h + 1) * LANES]
        ss = jnp.sum(t[0:A_QK] * t[0:A_QK], axis=0, keepdims=True)
        q_ref[0, h] = (t * lax.rsqrt(ss * (1.0 / A_QK) + EPS) * tqt).astype(BF16)

    lane = lax.broadcasted_iota(jnp.int32, (1, LANES), 1)
    lane_mask = (lane < A_QK).astype(F32)
    c_kv = _dot(hb, win_ref[:, OFF_CKV:OFF_CKV + KV_LORA])
    ckvn = c_kv * lax.rsqrt(jnp.mean(c_kv * c_kv, axis=-1, keepdims=True) + EPS) * kvag_ref[...]
    ckvn_b = ckvn.astype(BF16)
    kr = _dot(hb, win_ref[:, OFF_KR:OFF_KR + LANES])
    kr2 = _dot(hb, win_ref[:, OFF_KR2:OFF_KR2 + LANES]) * tk2_ref[...]
    tk1 = tk1_ref[...]
    for h in range(A_HEADS):
        t = _dot(ckvn_b, wuk_ref[:, h * LANES:(h + 1) * LANES]) + kr
        ss = jnp.sum(t * t * lane_mask, axis=-1, keepdims=True)
        k_ref[0, h] = ((t * tk1 + kr2) * lax.rsqrt(ss * (1.0 / A_QK) + EPS)).astype(BF16)
    vt = _dot(wuvt_ref[...], ckvn.T.astype(BF16))
    vt_ref[0, :, 0:A_V, :] = vt.astype(BF16).reshape(A_HEADS, A_V, tm)
    vt_ref[0, :, A_V:, :] = jnp.ones((A_HEADS, ONES_ROWS, tm), BF16)

    sag_ref[0] = _silu(_dot(hb, win_ref[:, OFF_AGATE:OFF_AGATE + A_WIDTH])).astype(BF16)


def _in_proj(x, scale, shift, lw, tabs, tm, n_keys, key_start, kv_bufs=None):
    b, n, d = x.shape
    key_blk = key_start // tm
    mod_map = (lambda bi, i: (bi, 0, 0)) if scale.shape[0] > 1 else (lambda bi, i: (0, 0, 0))
    const2 = lambda bi, i: (0, 0)
    row_tab = lambda bi, i: (i, 0)
    tok = lambda bi, i: (bi, i, 0)
    full = lambda a: pl.BlockSpec(a.shape, const2)
    in_specs = [
        pl.BlockSpec((1, tm, d), tok),
        pl.BlockSpec((1, 1, d), mod_map),
        pl.BlockSpec((1, 1, d), mod_map),
        full(lw["norm_g"]), full(lw["w_in"]), full(lw["jm"]),
        full(lw["g_ln_g"]), full(lw["ws_cat"]), full(lw["bs_tab"]), full(lw["q_a_g"]),
        full(lw["w_uq_t"]), full(lw["kv_a_g"]), full(lw["w_uk"]), full(lw["w_uv_t"]),
        pl.BlockSpec((LANES, tm), lambda bi, i: (0, i)),
        pl.BlockSpec((tm, LANES), row_tab), pl.BlockSpec((tm, LANES), row_tab),
    ]
    out_shape = [
        jax.ShapeDtypeStruct((b, n, F_WIDTH), BF16),
        jax.ShapeDtypeStruct((b, n, F_WIDTH), BF16),
        jax.ShapeDtypeStruct((b, n, G_WIDTH), BF16),
        jax.ShapeDtypeStruct((b, A_HEADS, LANES, n), BF16),
        jax.ShapeDtypeStruct((b, A_HEADS, n_keys, LANES), BF16),
        jax.ShapeDtypeStruct((b, A_HEADS, A_V + ONES_ROWS, n_keys), BF16),
        jax.ShapeDtypeStruct((b, n, A_WIDTH), BF16),
    ]
    out_specs = [
        pl.BlockSpec((1, tm, F_WIDTH), tok),
        pl.BlockSpec((1, tm, F_WIDTH), tok),
        pl.BlockSpec((1, tm, G_WIDTH), tok),
        pl.BlockSpec((1, A_HEADS, LANES, tm), lambda bi, i: (bi, 0, 0, i)),
        pl.BlockSpec((1, A_HEADS, tm, LANES), lambda bi, i: (bi, 0, key_blk + i, 0)),
        pl.BlockSpec((1, A_HEADS, A_V + ONES_ROWS, tm), lambda bi, i: (bi, 0, 0, key_blk + i)),
        pl.BlockSpec((1, tm, A_WIDTH), tok),
    ]
    args = [x, scale, shift, lw["norm_g"], lw["w_in"], lw["jm"], lw["g_ln_g"],
            lw["ws_cat"], lw["bs_tab"], lw["q_a_g"], lw["w_uq_t"], lw["kv_a_g"], lw["w_uk"], lw["w_uv_t"],
            tabs["qt"], tabs["k1"], tabs["k2"]]
    aliases = {}
    if kv_bufs is not None:
        aliases = {len(args): 4, len(args) + 1: 5}
        in_specs += [pl.BlockSpec(memory_space=pl.ANY), pl.BlockSpec(memory_space=pl.ANY)]
        args += list(kv_bufs)
    return pl.pallas_call(
        _in_proj_kernel,
        grid=(b, n // tm),
        in_specs=in_specs,
        out_specs=out_specs,
        out_shape=out_shape,
        input_output_aliases=aliases,
        compiler_params=pltpu.CompilerParams(
            dimension_semantics=("arbitrary", "arbitrary"), vmem_limit_bytes=VMEM_LIMIT_BYTES),
        name="in_proj",
    )(*args)


def _dft_stage1_kernel(w_ref, u_ref, ct_ref, st_ref, zr_ref, zi_ref):
    r = u_ref.shape[1]
    y = _dot(w_ref[...], u_ref[0])
    yr, yi = y[0:r], y[r:]
    ct, st = ct_ref[...], st_ref[...]
    zr_ref[0] = (yr * ct + yi * st).astype(BF16)
    zi_ref[0] = (yi * ct - yr * st).astype(BF16)


def _dft_stage2_kernel(zr_ref, zi_ref, d_ref, bc_ref, bs_ref, o_ref, *, norm):
    groups = o_ref.shape[2]
    xs = []
    for j in range(groups):
        rows = slice(j * GRID_W, (j + 1) * GRID_W)
        z = jnp.concatenate([zr_ref[0, rows, :], zi_ref[0, rows, :]], axis=0)
        xs.append(_dot(d_ref[...], z))
    xr = jnp.concatenate([x[0:GRID_W] for x in xs], axis=0).astype(BF16)
    xi = jnp.concatenate([x[GRID_W:] for x in xs], axis=0).astype(BF16)
    f = (_dot(xr, bc_ref[...]) + _dot(xi, bs_ref[...])) * norm
    for j in range(groups):
        o_ref[0, :, j, :] = f[j * GRID_W:(j + 1) * GRID_W]


def _pos_dft_factored(u, bc, bs):
    b, n, w = u.shape
    r = n // GRID_W
    lanes = GRID_W * w
    lt = 2048
    cos_r, sin_r = _dft_tables(r)
    w1 = jnp.concatenate([cos_r, -sin_r], axis=0).astype(BF16)
    ang = (jnp.arange(r, dtype=jnp.int32)[:, None] * jnp.arange(GRID_W, dtype=jnp.int32)[None, :]
           ).astype(F32) * (2.0 * math.pi / n)
    ct = jnp.repeat(jnp.cos(ang), w, axis=1)
    st = jnp.repeat(jnp.sin(ang), w, axis=1)
    zr, zi = pl.pallas_call(
        _dft_stage1_kernel,
        grid=(lanes // lt, b),
        in_specs=[pl.BlockSpec((2 * r, r), lambda j, bi: (0, 0)),
                  pl.BlockSpec((1, r, lt), lambda j, bi: (bi, 0, j)),
                  pl.BlockSpec((r, lt), lambda j, bi: (0, j)),
                  pl.BlockSpec((r, lt), lambda j, bi: (0, j))],
        out_specs=[pl.BlockSpec((1, r, lt), lambda j, bi: (bi, 0, j)),
                   pl.BlockSpec((1, r, lt), lambda j, bi: (bi, 0, j))],
        out_shape=[jax.ShapeDtypeStruct((b, r, lanes), BF16), jax.ShapeDtypeStruct((b, r, lanes), BF16)],
        compiler_params=pltpu.CompilerParams(
            dimension_semantics=("arbitrary", "arbitrary"), vmem_limit_bytes=VMEM_LIMIT_BYTES),
        name="dft_stage1",
    )(w1, u.reshape(b, r, lanes), ct, st)

    cos64, sin64 = _dft_tables(GRID_W)
    d = jnp.concatenate([jnp.concatenate([cos64, sin64], axis=1),
                         jnp.concatenate([-sin64, cos64], axis=1)], axis=0).astype(BF16)
    groups = 8
    f = pl.pallas_call(
        functools.partial(_dft_stage2_kernel, norm=1.0 / math.sqrt(n * F_GROUP_DIM)),
        grid=(b, r // groups),
        in_specs=[pl.BlockSpec((1, groups * GRID_W, w), lambda bi, g: (bi, g, 0)),
                  pl.BlockSpec((1, groups * GRID_W, w), lambda bi, g: (bi, g, 0)),
                  pl.BlockSpec(d.shape, lambda bi, g: (0, 0)),
                  pl.BlockSpec(bc.shape, lambda bi, g: (0, 0)),
                  pl.BlockSpec(bs.shape, lambda bi, g: (0, 0))],
        out_specs=pl.BlockSpec((1, GRID_W, groups, w), lambda bi, g: (bi, 0, g, 0)),
        out_shape=jax.ShapeDtypeStruct((b, GRID_W, r, w), F32),
        compiler_params=pltpu.CompilerParams(
            dimension_semantics=("arbitrary", "arbitrary"), vmem_limit_bytes=VMEM_LIMIT_BYTES),
        name="dft_stage2",
    )(zr.reshape(b, n, w), zi.reshape(b, n, w), d, bc, bs)
    return f.reshape(b, n, w)


def _dft_dense_kernel(cn_ref, sn_ref, u_ref, bc_ref, bs_ref, o_ref, *, norm):
    u = u_ref[0]
    pc = _dot(cn_ref[...], u).astype(BF16)
    ps = _dot(sn_ref[...], u).astype(BF16)
    o_ref[0] = (_dot(pc, bc_ref[...]) - _dot(ps, bs_ref[...])) * norm


def _pos_dft_dense(u, bc, bs):
    b, n, w = u.shape
    cos_n, sin_n = _dft_tables(n)
    const2 = lambda bi: (0, 0)
    return pl.pallas_call(
        functools.partial(_dft_dense_kernel, norm=1.0 / math.sqrt(n * F_GROUP_DIM)),
        grid=(b,),
        in_specs=[pl.BlockSpec((n, n), const2), pl.BlockSpec((n, n), const2),
                  pl.BlockSpec((1, n, w), lambda bi: (bi, 0, 0)),
                  pl.BlockSpec(bc.shape, const2), pl.BlockSpec(bs.shape, const2)],
        out_specs=pl.BlockSpec((1, n, w), lambda bi: (bi, 0, 0)),
        out_shape=jax.ShapeDtypeStruct((b, n, w), F32),
        compiler_params=pltpu.CompilerParams(
            dimension_semantics=("arbitrary",), vmem_limit_bytes=VMEM_LIMIT_BYTES),
        name="dft_dense",
    )(cos_n.astype(BF16), sin_n.astype(BF16), u, bc, bs)


def _pos_dft(u, bc, bs):
    r = u.shape[1] // GRID_W
    if r % 16 == 0:
        return _pos_dft_factored(u, bc, bs)
    return _pos_dft_dense(u, bc, bs)


def _attention_kernel(qt_ref, k_ref, vt_ref, o_ref, s0_ref, s1_ref, ot_ref):
    heads, tq = qt_ref.shape[1], qt_ref.shape[3]
    nchunks = k_ref.shape[2] // KEY_CHUNK
    s_bufs = (s0_ref, s1_ref)

    def phase(h_prod, h_cons, m_cons, prod_buf):
        m_part, acc = None, None
        for c in range(nchunks):
            rows = slice(c * KEY_CHUNK, (c + 1) * KEY_CHUNK)
            if h_prod is not None:
                s_t = _dot(k_ref[0, h_prod, rows, :], qt_ref[0, h_prod])
                s_bufs[prod_buf][rows, :] = s_t
                pm = jnp.max(s_t.reshape(KEY_CHUNK // 8, 8, tq), axis=0)
                m_part = pm if m_part is None else jnp.maximum(m_part, pm)
            if h_cons is not None:
                p_t = jnp.exp2(s_bufs[1 - prod_buf][rows, :] - m_cons).astype(BF16)
                d = _dot(vt_ref[0, h_cons, :, rows], p_t)
                acc = d if acc is None else acc + d
        if h_cons is not None:
            ot_ref[pl.ds(pl.multiple_of(h_cons * A_V, A_V), A_V), :] = acc[0:A_V] / acc[A_V:A_V + 1]
        return None if h_prod is None else jnp.max(m_part, axis=0, keepdims=True)

    def head_pair(j, m):
        m = phase(2 * j + 1, 2 * j, m, 1)
        return phase(2 * j + 2, 2 * j + 1, m, 0)

    m = phase(0, None, None, 0)
    m = lax.fori_loop(0, (heads - 2) // 2, head_pair, m)
    m = phase(heads - 1, heads - 2, m, 1)
    phase(None, heads - 1, m, 0)
    o_ref[0] = ot_ref[...].T.astype(BF16)


def _attention(qt, k_all, vt_all, key_start, n_keys, tq):
    b, heads, _, n = qt.shape
    key_blk = key_start // n_keys
    resident = pl.Buffered(1)
    return pl.pallas_call(
        _attention_kernel,
        grid=(b, n // tq),
        in_specs=[pl.BlockSpec((1, heads, LANES, tq), lambda bi, qi: (bi, 0, 0, qi)),
                  pl.BlockSpec((1, heads, n_keys, LANES), lambda bi, qi: (bi, 0, key_blk, 0),
                               pipeline_mode=resident),
                  pl.BlockSpec((1, heads, A_V + ONES_ROWS, n_keys), lambda bi, qi: (bi, 0, 0, key_blk),
                               pipeline_mode=resident)],
        out_specs=pl.BlockSpec((1, tq, heads * A_V), lambda bi, qi: (bi, qi, 0)),
        out_shape=jax.ShapeDtypeStruct((b, n, heads * A_V), BF16),
        scratch_shapes=[pltpu.VMEM((n_keys, tq), F32), pltpu.VMEM((n_keys, tq), F32),
                        pltpu.VMEM((heads * A_V, tq), F32)],
        compiler_params=pltpu.CompilerParams(
            dimension_semantics=("arbitrary", "arbitrary"),
            vmem_limit_bytes=VMEM_LIMIT_BYTES),
        name="attention",
    )(qt, k_all, vt_all)


def _out_proj_kernel(x_ref, gate_ref, f_ref, sfg_ref, gout_ref, o_ref, sag_ref, wf_ref, wo_ref, y_ref):
    f_out = (_dot(f_ref[0].astype(BF16), wf_ref[...]) * sfg_ref[0].astype(F32)).astype(BF16)
    o_g = (o_ref[0].astype(F32) * sag_ref[0].astype(F32)).astype(BF16)
    y = (_dot(f_out, wo_ref[0:F_WIDTH, :])
         + _dot(gout_ref[0], wo_ref[F_WIDTH:F_WIDTH + G_WIDTH, :])
         + _dot(o_g, wo_ref[F_WIDTH + G_WIDTH:, :]))
    y_ref[0] = x_ref[0] + gate_ref[0] * y


def _out_proj(x, gate, f, sfg, gout, o, sag, w_fmix, w_out, tm):
    b, n, d = x.shape
    tok = lambda bi, i: (bi, i, 0)
    mod_map = (lambda bi, i: (bi, 0, 0)) if gate.shape[0] > 1 else (lambda bi, i: (0, 0, 0))
    const2 = lambda bi, i: (0, 0)
    return pl.pallas_call(
        _out_proj_kernel,
        grid=(b, n // tm),
        in_specs=[pl.BlockSpec((1, tm, d), tok),
                  pl.BlockSpec((1, 1, d), mod_map),
                  pl.BlockSpec((1, tm, F_WIDTH), tok),
                  pl.BlockSpec((1, tm, F_WIDTH), tok),
                  pl.BlockSpec((1, tm, G_WIDTH), tok),
                  pl.BlockSpec((1, tm, A_WIDTH), tok),
                  pl.BlockSpec((1, tm, A_WIDTH), tok),
                  pl.BlockSpec(w_fmix.shape, const2),
                  pl.BlockSpec(w_out.shape, const2)],
        out_specs=pl.BlockSpec((1, tm, d), tok),
        out_shape=jax.ShapeDtypeStruct((b, n, d), F32),
        compiler_params=pltpu.CompilerParams(
            dimension_semantics=("arbitrary", "arbitrary"), vmem_limit_bytes=VMEM_LIMIT_BYTES),
        name="out_proj",
    )(x, gate, f, sfg, gout, o, sag, w_fmix, w_out)


def _rot_perm_sign():
    q = A_ROPE // 4
    src, sign = [], []
    for j in range(A_ROPE):
        blk, r = divmod(j, q)
        if blk % 2 == 0:
            src.append((blk + 1) * q + r)
            sign.append(-1.0)
        else:
            src.append((blk - 1) * q + r)
            sign.append(1.0)
    return jnp.array(src, jnp.int32), jnp.array(sign, F32)


def _dft_tables(n):
    idx = (jnp.arange(n, dtype=jnp.int32)[:, None] * jnp.arange(n, dtype=jnp.int32)[None, :]) % n
    ang = idx.astype(F32) * (2.0 * math.pi / n)
    return jnp.cos(ang), jnp.sin(ang)


def _rope_cos_sin(n):
    rows = n // GRID_W
    row = jnp.repeat(jnp.arange(rows, dtype=F32), GRID_W)
    col = jnp.tile(jnp.arange(GRID_W, dtype=F32), rows)
    half = A_ROPE // 2
    inv = ROPE_BASE ** (-jnp.arange(0, half, 2, dtype=F32) / half)
    ang_r = row[:, None] * inv[None, :]
    ang_c = col[:, None] * inv[None, :]
    ang = jnp.concatenate([ang_r, ang_r, ang_c, ang_c], axis=-1)
    return jnp.cos(ang), jnp.sin(ang)


def _head_tables(gain, cos, sin, src):
    n = cos.shape[0]
    g_nope = jnp.broadcast_to(gain[:A_NOPE][None, :], (n, A_NOPE))
    a = gain[A_NOPE:][None, :] * cos
    b = gain[A_NOPE:][src][None, :] * sin
    tab1 = jnp.concatenate([g_nope, a, b], axis=-1)
    tab2 = jnp.concatenate([jnp.zeros((n, A_NOPE), F32), b, a], axis=-1)
    return tab1, tab2


def _layer_weights(l, p, src, sign):
    w_in = p["w_in"][l]
    d = w_in.shape[0]
    sizes = (F_WIDTH, F_WIDTH, G_WIDTH, G_WIDTH, G_WIDTH, Q_LORA, KV_LORA, A_ROPE, A_WIDTH)
    parts, start = [], 0
    for s in sizes:
        parts.append(w_in[:, start:start + s])
        start += s
    w_kr = parts[7]
    w_rot = w_kr[:, src] * sign[None, :]
    kr_tile = jnp.concatenate([jnp.zeros((d, A_NOPE), F32), w_kr, w_rot], axis=-1)
    kr2_tile = jnp.concatenate([jnp.zeros((d, A_NOPE), F32), w_rot, w_kr], axis=-1)
    w_in_p = jnp.concatenate(parts[:7] + [kr_tile, kr2_tile, parts[8]], axis=-1).astype(BF16)

    w_uq = p["w_uq"][l].reshape(Q_LORA, A_HEADS, A_QK)
    w_uq_rope = w_uq[:, :, A_NOPE:]
    w_uq_p = jnp.concatenate([w_uq, w_uq_rope[:, :, src] * sign[None, None, :]], axis=-1)
    w_uq_t = w_uq_p.reshape(Q_LORA, A_HEADS * LANES).T.astype(BF16)

    w_ukv = p["w_ukv"][l].reshape(KV_LORA, A_HEADS, A_NOPE + A_V)
    w_uk_p = jnp.concatenate([w_ukv[:, :, :A_NOPE], jnp.zeros((KV_LORA, A_HEADS, LANES - A_NOPE), F32)], axis=-1)
    w_uk_p = w_uk_p.reshape(KV_LORA, A_HEADS * LANES).astype(BF16)
    w_uv_t = w_ukv[:, :, A_NOPE:].reshape(KV_LORA, A_WIDTH).T.astype(BF16)

    ws = p["g_ws"][l]
    ws_cat = jnp.transpose(ws, (1, 0, 2)).reshape(CHUNK, G_HEADS * CHUNK).astype(BF16)
    bs_tab = jnp.repeat(p["g_bs"][l].T, G_HEAD_DIM, axis=1)
    return {
        "norm_g": p["norm_g"][l][None, :], "w_in": w_in_p, "w_uq_t": w_uq_t, "w_uk": w_uk_p, "w_uv_t": w_uv_t,
        "g_ln_g": jnp.tile(p["g_ln_g"][l], G_HEADS)[None, :], "ws_cat": ws_cat, "bs_tab": bs_tab,
        "q_a_g": p["q_a_g"][l][None, :], "kv_a_g": p["kv_a_g"][l][None, :],
        "w_fmix": p["w_fmix"][l].astype(BF16), "w_out": p["w_out"][l].astype(BF16),
    }


def _token_tile(n):
    return 512 if n % 512 == 0 else 256


def kernel(x, c, ctx, c_ctx, w_mod, b_mod, norm_g, w_in, w_fmix, g_ln_g, g_ws, g_bs,
           q_a_g, w_uq, kv_a_g, w_ukv, q_norm_g, k_norm_g, w_out):
    p = dict(w_in=w_in, w_fmix=w_fmix, g_ln_g=g_ln_g, g_ws=g_ws, g_bs=g_bs, q_a_g=q_a_g, w_uq=w_uq,
             kv_a_g=kv_a_g, w_ukv=w_ukv, norm_g=norm_g, w_out=w_out)
    depth = w_mod.shape[0]
    b, n, d = x.shape
    n_ctx = ctx.shape[1]
    src, sign = _rot_perm_sign()

    rows = -(-(b + 1) // 8) * 8
    cc = jnp.concatenate([c, c_ctx[None, :], jnp.zeros((rows - b - 1, d), F32)], axis=0)
    mod = _modulation(_silu(cc), w_mod, b_mod)

    cos64, sin64 = _dft_tables(F_GROUP_DIM)
    eye_g = jnp.eye(F_GROUPS, dtype=F32)
    bc = jnp.kron(eye_g, cos64).astype(BF16)
    bs = jnp.kron(eye_g, sin64).astype(BF16)
    jm = jnp.kron(jnp.eye(G_HEADS, dtype=F32), jnp.full((G_HEAD_DIM, G_HEAD_DIM), 1.0 / G_HEAD_DIM, F32)).astype(BF16)
    cos_x, sin_x = _rope_cos_sin(n)
    cos_y, sin_y = jnp.ones((n_ctx, A_ROPE), F32), jnp.zeros((n_ctx, A_ROPE), F32)
    q_scale = (A_QK ** -0.5) * math.log2(math.e)

    y = ctx
    for l in range(depth):
        lw = _layer_weights(l, p, src, sign)
        lw.update(jm=jm)
        shift, scale, gate = (mod[l, :, i * d:(i + 1) * d] for i in range(3))
        tabs_x, tabs_y = {}, {}
        for tabs, cs, sn in ((tabs_x, cos_x, sin_x), (tabs_y, cos_y, sin_y)):
            tabs["qt"] = (_head_tables(q_norm_g[l], cs, sn, src)[0] * q_scale).T
            tabs["k1"], tabs["k2"] = _head_tables(k_norm_g[l], cs, sn, src)

        tm_y, tm_x = _token_tile(n_ctx), _token_tile(n)
        n_keys = n + n_ctx
        fin_y, sfg_y, gout_y, q_y, k_all, vt_all, sag_y = _in_proj(
            y, scale[b:b + 1, None, :], shift[b:b + 1, None, :], lw, tabs_y, tm_y, n_keys, n)
        fin_x, sfg_x, gout_x, q_x, k_all, vt_all, sag_x = _in_proj(
            x, scale[:b, None, :], shift[:b, None, :], lw, tabs_x, tm_x, n_keys, 0, (k_all, vt_all))

        f_x = _pos_dft(fin_x, bc, bs)
        o_x = _attention(q_x, k_all, vt_all, 0, n_keys, tm_x)
        x = _out_proj(x, gate[:b, None, :], f_x, sfg_x, gout_x, o_x, sag_x, lw["w_fmix"], lw["w_out"], tm_x)
        if l < depth - 1:
            f_y = _pos_dft(fin_y, bc, bs)
            o_y = _attention(q_y, k_all, vt_all, n, n_ctx, tm_y)
            y = _out_proj(y, gate[b:b + 1, None, :], f_y, sfg_y, gout_y, o_y, sag_y,
                          lw["w_fmix"], lw["w_out"], tm_y)
    return x
```

```python
import functools
import math

import jax
import jax.numpy as jnp
from jax import lax
from jax.experimental import pallas as pl
from jax.experimental.pallas import tpu as pltpu

F32 = jnp.float32
BF16 = jnp.bfloat16

LANES = 128
VMEM_LIMIT_BYTES = 56 * 1024 * 1024

GRID_W = 64
F_GROUPS = 4
F_GROUP_DIM = 64
F_WIDTH = F_GROUPS * F_GROUP_DIM
G_HEADS = 4
G_HEAD_DIM = 64
G_WIDTH = G_HEADS * G_HEAD_DIM
CHUNK = 128
A_HEADS = 8
A_NOPE = 64
A_ROPE = 32
A_V = 64
A_QK = A_NOPE + A_ROPE
A_WIDTH = A_HEADS * A_V
Q_LORA = 256
KV_LORA = 128
ROPE_BASE = 10000.0
EPS = 1e-6
ONES_ROWS = 16
KEY_CHUNK = 256
MAX_SAFE_SHIFT = 60.0
HEAD_GROUP = 8
LOOKAHEAD = 2

OFF_FIN = 0
OFF_FGATE = OFF_FIN + F_WIDTH
OFF_GU = OFF_FGATE + F_WIDTH
OFF_GV = OFF_GU + G_WIDTH
OFF_GGATE = OFF_GV + G_WIDTH
OFF_CQ = OFF_GGATE + G_WIDTH
OFF_CKV = OFF_CQ + Q_LORA
OFF_KR = OFF_CKV + KV_LORA
OFF_KR2 = OFF_KR + LANES
OFF_AGATE = OFF_KR2 + LANES
IN_P = OFF_AGATE + A_WIDTH


def _silu(t):
    return t * (1.0 / (1.0 + jnp.exp(-t)))


def _dot(a, b):
    return jnp.dot(a, b, preferred_element_type=F32)


def _dot_hilo(t, w):
    hi = t.astype(BF16)
    lo = (t - hi.astype(F32)).astype(BF16)
    return _dot(hi, w) + _dot(lo, w)


def _mod_kernel(cc_ref, w_ref, b_ref, o_ref):
    o_ref[0] = lax.dot_general(cc_ref[...], w_ref[0], (((1,), (0,)), ((), ())),
                               precision=lax.Precision.HIGHEST,
                               preferred_element_type=F32) + b_ref[0]


def _modulation(cc, w_mod, b_mod):
    depth, d, d3 = w_mod.shape
    rows = cc.shape[0]
    tn = 1024
    return pl.pallas_call(
        _mod_kernel,
        grid=(depth, d3 // tn),
        in_specs=[pl.BlockSpec((rows, d), lambda l, j: (0, 0)),
                  pl.BlockSpec((1, d, tn), lambda l, j: (l, 0, j)),
                  pl.BlockSpec((1, 1, tn), lambda l, j: (l, 0, j))],
        out_specs=pl.BlockSpec((1, rows, tn), lambda l, j: (l, 0, j)),
        out_shape=jax.ShapeDtypeStruct((depth, rows, d3), F32),
        compiler_params=pltpu.CompilerParams(vmem_limit_bytes=VMEM_LIMIT_BYTES),
        name="modulation",
    )(cc, w_mod, b_mod.reshape(depth, 1, d3))


def _in_proj_kernel(x_ref, sc_ref, sh_ref, ng_ref, win_ref, jm_ref, gln_ref,
                    ws_ref, bsb_ref, qag_ref, wuqt_ref, kvag_ref, wuk_ref, wuvt_ref,
                    tqt_ref, tk1_ref, tk2_ref, *rest):
    fin_ref, sfg_ref, gout_ref, q_ref, k_ref, vt_ref, sag_ref = rest[-7:]
    tm = x_ref.shape[1]
    x = x_ref[0]
    xn = x * lax.rsqrt(jnp.mean(x * x, axis=-1, keepdims=True) + EPS) * ng_ref[...]
    hb = (xn * (1.0 + sc_ref[0]) + sh_ref[0]).astype(BF16)

    def proj(off, width):
        return _dot(hb, win_ref[:, off:off + width])

    g_v = proj(OFF_GV, G_WIDTH)
    c_q = proj(OFF_CQ, Q_LORA)
    c_kv = proj(OFF_CKV, KV_LORA)
    kr = proj(OFF_KR, LANES)
    kr2 = proj(OFF_KR2, LANES) * tk2_ref[...]
    fin_ref[0] = proj(OFF_FIN, F_WIDTH).astype(BF16)
    jm = jm_ref[...]
    dv = g_v - _dot_hilo(g_v, jm)
    sfg_ref[0] = _silu(proj(OFF_FGATE, F_WIDTH)).astype(BF16)
    var = _dot_hilo(dv * dv, jm)
    g_u = proj(OFF_GU, G_WIDTH)
    g_gate = proj(OFF_GGATE, G_WIDTH)

    cqn = c_q * lax.rsqrt(jnp.mean(c_q * c_q, axis=-1, keepdims=True) + EPS) * qag_ref[...]
    qt_all = _dot(wuqt_ref[...], cqn.T.astype(BF16))
    tqt = tqt_ref[...]
    for h in range(A_HEADS):
        t = qt_all[h * LANES:(h + 1) * LANES]
        ss = jnp.sum(t[0:A_QK] * t[0:A_QK], axis=0, keepdims=True)
        q_ref[0, h] = (t * lax.rsqrt(ss * (1.0 / A_QK) + EPS) * tqt).astype(BF16)

    lane = lax.broadcasted_iota(jnp.int32, (1, LANES), 1)
    lane_mask = (lane < A_QK).astype(F32)
    ckvn = c_kv * lax.rsqrt(jnp.mean(c_kv * c_kv, axis=-1, keepdims=True) + EPS) * kvag_ref[...]
    ckvn_b = ckvn.astype(BF16)
    tk1 = tk1_ref[...]
    for h in range(A_HEADS):
        t = _dot(ckvn_b, wuk_ref[:, h * LANES:(h + 1) * LANES]) + kr
        ss = jnp.sum(t * t * lane_mask, axis=-1, keepdims=True)
        k_ref[0, h] = ((t * tk1 + kr2) * lax.rsqrt(ss * (1.0 / A_QK) + EPS)).astype(BF16)
    vt = _dot(wuvt_ref[...], ckvn.T.astype(BF16))
    vt_ref[0, :, 0:A_V, :] = vt.astype(BF16).reshape(A_HEADS, A_V, tm)
    vt_ref[0, :, A_V:, :] = jnp.ones((A_HEADS, ONES_ROWS, tm), BF16)

    sag_ref[0] = _silu(proj(OFF_AGATE, A_WIDTH)).astype(BF16)

    vln = dv * lax.rsqrt(var + EPS) * gln_ref[...]
    head_of_lane = lax.broadcasted_iota(jnp.int32, (CHUNK, G_WIDTH), 1) // G_HEAD_DIM
    for c in range(tm // CHUNK):
        rows = slice(c * CHUNK, (c + 1) * CHUNK)
        vc = vln[rows]
        stacked = jnp.concatenate(
            [jnp.where(head_of_lane == h, vc, 0.0).astype(BF16) for h in range(G_HEADS)], axis=0)
        mixed = _dot(ws_ref[...], stacked) + bsb_ref[...]
        gout_ref[0, rows, :] = (g_u[rows] * mixed * _silu(g_gate[rows])).astype(BF16)


def _in_proj(x, scale, shift, lw, tabs, tm, n_keys, key_start, kv_bufs=None):
    b, n, d = x.shape
    key_blk = key_start // tm
    mod_map = (lambda bi, i: (bi, 0, 0)) if scale.shape[0] > 1 else (lambda bi, i: (0, 0, 0))
    const2 = lambda bi, i: (0, 0)
    row_tab = lambda bi, i: (i, 0)
    tok = lambda bi, i: (bi, i, 0)
    full = lambda a: pl.BlockSpec(a.shape, const2)
    in_specs = [
        pl.BlockSpec((1, tm, d), tok),
        pl.BlockSpec((1, 1, d), mod_map),
        pl.BlockSpec((1, 1, d), mod_map),
        full(lw["norm_g"]), full(lw["w_in"]), full(lw["jm"]),
        full(lw["g_ln_g"]), full(lw["ws_cat"]), full(lw["bs_tab"]), full(lw["q_a_g"]),
        full(lw["w_uq_t"]), full(lw["kv_a_g"]), full(lw["w_uk"]), full(lw["w_uv_t"]),
        pl.BlockSpec((LANES, tm), lambda bi, i: (0, i)),
        pl.BlockSpec((tm, LANES), row_tab), pl.BlockSpec((tm, LANES), row_tab),
    ]
    out_shape = [
        jax.ShapeDtypeStruct((b, n, F_WIDTH), BF16),
        jax.ShapeDtypeStruct((b, n, F_WIDTH), BF16),
        jax.ShapeDtypeStruct((b, n, G_WIDTH), BF16),
        jax.ShapeDtypeStruct((b, A_HEADS, LANES, n), BF16),
        jax.ShapeDtypeStruct((b, A_HEADS, n_keys, LANES), BF16),
        jax.ShapeDtypeStruct((b, A_HEADS, A_V + ONES_ROWS, n_keys), BF16),
        jax.ShapeDtypeStruct((b, n, A_WIDTH), BF16),
    ]
    out_specs = [
        pl.BlockSpec((1, tm, F_WIDTH), tok),
        pl.BlockSpec((1, tm, F_WIDTH), tok),
        pl.BlockSpec((1, tm, G_WIDTH), tok),
        pl.BlockSpec((1, A_HEADS, LANES, tm), lambda bi, i: (bi, 0, 0, i)),
        pl.BlockSpec((1, A_HEADS, tm, LANES), lambda bi, i: (bi, 0, key_blk + i, 0)),
        pl.BlockSpec((1, A_HEADS, A_V + ONES_ROWS, tm), lambda bi, i: (bi, 0, 0, key_blk + i)),
        pl.BlockSpec((1, tm, A_WIDTH), tok),
    ]
    args = [x, scale, shift, lw["norm_g"], lw["w_in"], lw["jm"], lw["g_ln_g"],
            lw["ws_cat"], lw["bs_tab"], lw["q_a_g"], lw["w_uq_t"], lw["kv_a_g"], lw["w_uk"], lw["w_uv_t"],
            tabs["qt"], tabs["k1"], tabs["k2"]]
    aliases = {}
    if kv_bufs is not None:
        aliases = {len(args): 4, len(args) + 1: 5}
        in_specs += [pl.BlockSpec(memory_space=pl.ANY), pl.BlockSpec(memory_space=pl.ANY)]
        args += list(kv_bufs)
    return pl.pallas_call(
        _in_proj_kernel,
        grid=(b, n // tm),
        in_specs=in_specs,
        out_specs=out_specs,
        out_shape=out_shape,
        input_output_aliases=aliases,
        compiler_params=pltpu.CompilerParams(
            dimension_semantics=("arbitrary", "arbitrary"), vmem_limit_bytes=VMEM_LIMIT_BYTES),
        name="in_proj",
    )(*args)


DFT_T = 16


def _dft_stage1_kernel(wc_ref, ws_ref, u_ref, ct_ref, st_ref, zr_ref, zi_ref):
    r, t, w = u_ref.shape[1:]
    u = u_ref[0].reshape(r * t, w)
    yr = _dot(wc_ref[...], u)
    yi = _dot(ws_ref[...], u)
    ct, st = ct_ref[...].reshape(r * t, w), st_ref[...].reshape(r * t, w)
    zr_ref[0] = (yr * ct + yi * st).astype(BF16).reshape(r, t, w)
    zi_ref[0] = (yi * ct - yr * st).astype(BF16).reshape(r, t, w)


def _dft_stage2_kernel(zr_ref, zi_ref, d_ref, bc_ref, bs_ref, o_ref, *, norm):
    groups = o_ref.shape[2]
    xs = []
    for j in range(groups):
        rows = slice(j * GRID_W, (j + 1) * GRID_W)
        z = jnp.concatenate([zr_ref[0, rows, :], zi_ref[0, rows, :]], axis=0)
        xs.append(_dot(d_ref[...], z))
    xr = jnp.concatenate([x[0:GRID_W] for x in xs], axis=0).astype(BF16)
    xi = jnp.concatenate([x[GRID_W:] for x in xs], axis=0).astype(BF16)
    f = (_dot(xr, bc_ref[...]) + _dot(xi, bs_ref[...])) * norm
    for j in range(groups):
        o_ref[0, :, j, :] = f[j * GRID_W:(j + 1) * GRID_W]


def _pos_dft_factored(u, bc, bs):
    b, n, w = u.shape
    r = n // GRID_W
    cos_r, sin_r = _dft_tables(r)
    eye_t = jnp.eye(DFT_T, dtype=F32)
    wc = jnp.kron(cos_r, eye_t).astype(BF16)
    ws = jnp.kron(-sin_r, eye_t).astype(BF16)
    ang = (jnp.arange(r, dtype=jnp.int32)[:, None] * jnp.arange(GRID_W, dtype=jnp.int32)[None, :]
           ).astype(F32) * (2.0 * math.pi / n)
    ct = jnp.broadcast_to(jnp.cos(ang)[:, :, None], (r, GRID_W, w))
    st = jnp.broadcast_to(jnp.sin(ang)[:, :, None], (r, GRID_W, w))
    blk = lambda j, bi: (bi, 0, j, 0)
    zr, zi = pl.pallas_call(
        _dft_stage1_kernel,
        grid=(GRID_W // DFT_T, b),
        in_specs=[pl.BlockSpec(wc.shape, lambda j, bi: (0, 0)),
                  pl.BlockSpec(ws.shape, lambda j, bi: (0, 0)),
                  pl.BlockSpec((1, r, DFT_T, w), blk),
                  pl.BlockSpec((r, DFT_T, w), lambda j, bi: (0, j, 0)),
                  pl.BlockSpec((r, DFT_T, w), lambda j, bi: (0, j, 0))],
        out_specs=[pl.BlockSpec((1, r, DFT_T, w), blk), pl.BlockSpec((1, r, DFT_T, w), blk)],
        out_shape=[jax.ShapeDtypeStruct((b, r, GRID_W, w), BF16)] * 2,
        compiler_params=pltpu.CompilerParams(
            dimension_semantics=("arbitrary", "arbitrary"), vmem_limit_bytes=VMEM_LIMIT_BYTES),
        name="dft_stage1",
    )(wc, ws, u.reshape(b, r, GRID_W, w), ct, st)

    cos64, sin64 = _dft_tables(GRID_W)
    d = jnp.concatenate([jnp.concatenate([cos64, sin64], axis=1),
                         jnp.concatenate([-sin64, cos64], axis=1)], axis=0).astype(BF16)
    groups = min(32, r)
    f = pl.pallas_call(
        functools.partial(_dft_stage2_kernel, norm=1.0 / math.sqrt(n * F_GROUP_DIM)),
        grid=(b, r // groups),
        in_specs=[pl.BlockSpec((1, groups * GRID_W, w), lambda bi, g: (bi, g, 0)),
                  pl.BlockSpec((1, groups * GRID_W, w), lambda bi, g: (bi, g, 0)),
                  pl.BlockSpec(d.shape, lambda bi, g: (0, 0)),
                  pl.BlockSpec(bc.shape, lambda bi, g: (0, 0)),
                  pl.BlockSpec(bs.shape, lambda bi, g: (0, 0))],
        out_specs=pl.BlockSpec((1, GRID_W, groups, w), lambda bi, g: (bi, 0, g, 0)),
        out_shape=jax.ShapeDtypeStruct((b, GRID_W, r, w), F32),
        compiler_params=pltpu.CompilerParams(
            dimension_semantics=("arbitrary", "arbitrary"), vmem_limit_bytes=VMEM_LIMIT_BYTES),
        name="dft_stage2",
    )(zr.reshape(b, n, w), zi.reshape(b, n, w), d, bc, bs)
    return f.reshape(b, n, w)


def _dft_dense_kernel(cn_ref, sn_ref, u_ref, bc_ref, bs_ref, o_ref, *, norm):
    u = u_ref[0]
    pc = _dot(cn_ref[...], u).astype(BF16)
    ps = _dot(sn_ref[...], u).astype(BF16)
    o_ref[0] = (_dot(pc, bc_ref[...]) - _dot(ps, bs_ref[...])) * norm


def _pos_dft_dense(u, bc, bs):
    b, n, w = u.shape
    cos_n, sin_n = _dft_tables(n)
    const2 = lambda bi: (0, 0)
    return pl.pallas_call(
        functools.partial(_dft_dense_kernel, norm=1.0 / math.sqrt(n * F_GROUP_DIM)),
        grid=(b,),
        in_specs=[pl.BlockSpec((n, n), const2), pl.BlockSpec((n, n), const2),
                  pl.BlockSpec((1, n, w), lambda bi: (bi, 0, 0)),
                  pl.BlockSpec(bc.shape, const2), pl.BlockSpec(bs.shape, const2)],
        out_specs=pl.BlockSpec((1, n, w), lambda bi: (bi, 0, 0)),
        out_shape=jax.ShapeDtypeStruct((b, n, w), F32),
        compiler_params=pltpu.CompilerParams(
            dimension_semantics=("arbitrary",), vmem_limit_bytes=VMEM_LIMIT_BYTES),
        name="dft_dense",
    )(cos_n.astype(BF16), sin_n.astype(BF16), u, bc, bs)


def _pos_dft(u, bc, bs):
    r = u.shape[1] // GRID_W
    if r % 16 == 0:
        return _pos_dft_factored(u, bc, bs)
    return _pos_dft_dense(u, bc, bs)


def _attention_kernel(qt_ref, k_ref, vt_ref, o_ref, s0_ref, s1_ref, ot_ref):
    heads, tq = qt_ref.shape[1], qt_ref.shape[3]
    nchunks = k_ref.shape[2] // KEY_CHUNK
    s_bufs = (s0_ref, s1_ref)

    def phase(h_prod, h_cons, m_cons, prod_buf):
        m_part, acc = None, None
        for c in range(nchunks):
            rows = slice(c * KEY_CHUNK, (c + 1) * KEY_CHUNK)
            if h_prod is not None:
                s_t = _dot(k_ref[0, h_prod, rows, :], qt_ref[0, h_prod])
                s_bufs[prod_buf][rows, :] = s_t
                pm = jnp.max(s_t.reshape(KEY_CHUNK // 8, 8, tq), axis=0)
                m_part = pm if m_part is None else jnp.maximum(m_part, pm)
            if h_cons is not None:
                p_t = jnp.exp2(s_bufs[1 - prod_buf][rows, :] - m_cons).astype(BF16)
                d = _dot(vt_ref[0, h_cons, :, rows], p_t)
                acc = d if acc is None else acc + d
        if h_cons is not None:
            ot_ref[pl.ds(pl.multiple_of(h_cons * A_V, A_V), A_V), :] = acc[0:A_V] / acc[A_V:A_V + 1]
        return None if h_prod is None else jnp.max(m_part, axis=0, keepdims=True)

    def head_pair(j, m):
        m = phase(2 * j + 1, 2 * j, m, 1)
        return phase(2 * j + 2, 2 * j + 1, m, 0)

    m = phase(0, None, None, 0)
    m = lax.fori_loop(0, (heads - 2) // 2, head_pair, m)
    m = phase(heads - 1, heads - 2, m, 1)
    phase(None, heads - 1, m, 0)
    o_ref[0] = ot_ref[...].T.astype(BF16)


def _attention_bounded_kernel(shift_ref, qt_ref, k_ref, vt_ref, o_ref, ot_ref):
    heads, tq = qt_ref.shape[1], qt_ref.shape[3]
    nchunks = k_ref.shape[2] // KEY_CHUNK
    shift = shift_ref[0, 0]

    def rows(c):
        return slice(c * KEY_CHUNK, (c + 1) * KEY_CHUNK)

    def head_group(j, carry):
        items = [(j * HEAD_GROUP + g, c) for g in range(HEAD_GROUP) for c in range(nchunks)]
        scores, acc = [], None
        for i in range(len(items) + LOOKAHEAD):
            if i < len(items):
                h, c = items[i]
                scores.append(_dot(k_ref[0, h, rows(c), :], qt_ref[0, h]))
            if i >= LOOKAHEAD:
                h, c = items[i - LOOKAHEAD]
                d = _dot(vt_ref[0, h, :, rows(c)], jnp.exp2(scores.pop(0) - shift).astype(BF16))
                acc = d if c == 0 else acc + d
                if c == nchunks - 1:
                    ot_ref[pl.ds(pl.multiple_of(h * A_V, A_V), A_V), :] = acc[0:A_V] / acc[A_V:A_V + 1]
        return carry

    lax.fori_loop(0, heads // HEAD_GROUP, head_group, 0)
    o_ref[0] = ot_ref[...].T.astype(BF16)


def _attention(qt, k_all, vt_all, key_start, n_keys, tq, score_bound):
    b, heads, _, n = qt.shape
    key_blk = key_start // n_keys
    q_spec = pl.BlockSpec((1, heads, LANES, tq), lambda bi, qi: (bi, 0, 0, qi))
    o_spec = pl.BlockSpec((1, tq, heads * A_V), lambda bi, qi: (bi, qi, 0))
    o_shape = jax.ShapeDtypeStruct((b, n, heads * A_V), BF16)
    params = pltpu.CompilerParams(dimension_semantics=("arbitrary", "arbitrary"),
                                  vmem_limit_bytes=VMEM_LIMIT_BYTES)

    def bounded(qt, k_all, vt_all):
        return pl.pallas_call(
            _attention_bounded_kernel,
            grid=(b, n // tq),
            in_specs=[pl.BlockSpec(memory_space=pltpu.SMEM), q_spec,
                      pl.BlockSpec((1, heads, n_keys, LANES), lambda bi, qi: (bi, 0, key_blk, 0)),
                      pl.BlockSpec((1, heads, A_V + ONES_ROWS, n_keys), lambda bi, qi: (bi, 0, 0, key_blk))],
            out_specs=o_spec,
            out_shape=o_shape,
            scratch_shapes=[pltpu.VMEM((heads * A_V, tq), F32)],
            compiler_params=params,
            name="attention_bounded",
        )(jnp.reshape(score_bound, (1, 1)), qt, k_all, vt_all)

    def row_max(qt, k_all, vt_all):
        return _attention_row_max(qt, k_all, vt_all, key_blk, n_keys, tq, q_spec, o_spec, o_shape, params)

    return lax.cond(score_bound <= MAX_SAFE_SHIFT, bounded, row_max, qt, k_all, vt_all)


def _attention_row_max(qt, k_all, vt_all, key_blk, n_keys, tq, q_spec, o_spec, o_shape, params):
    b, heads, _, n = qt.shape
    resident = pl.Buffered(1)
    return pl.pallas_call(
        _attention_kernel,
        grid=(b, n // tq),
        in_specs=[q_spec,
                  pl.BlockSpec((1, heads, n_keys, LANES), lambda bi, qi: (bi, 0, key_blk, 0),
                               pipeline_mode=resident),
                  pl.BlockSpec((1, heads, A_V + ONES_ROWS, n_keys), lambda bi, qi: (bi, 0, 0, key_blk),
                               pipeline_mode=resident)],
        out_specs=o_spec,
        out_shape=o_shape,
        scratch_shapes=[pltpu.VMEM((n_keys, tq), F32), pltpu.VMEM((n_keys, tq), F32),
                        pltpu.VMEM((heads * A_V, tq), F32)],
        compiler_params=params,
        name="attention",
    )(qt, k_all, vt_all)


def _out_proj_kernel(x_ref, gate_ref, f_ref, sfg_ref, gout_ref, o_ref, sag_ref, wf_ref, wo_ref, y_ref):
    f_out = (_dot(f_ref[0].astype(BF16), wf_ref[...]) * sfg_ref[0].astype(F32)).astype(BF16)
    o_g = (o_ref[0].astype(F32) * sag_ref[0].astype(F32)).astype(BF16)
    y = (_dot(f_out, wo_ref[0:F_WIDTH, :])
         + _dot(gout_ref[0], wo_ref[F_WIDTH:F_WIDTH + G_WIDTH, :])
         + _dot(o_g, wo_ref[F_WIDTH + G_WIDTH:, :]))
    y_ref[0] = x_ref[0] + gate_ref[0] * y


def _out_proj(x, gate, f, sfg, gout, o, sag, w_fmix, w_out, tm):
    b, n, d = x.shape
    tok = lambda bi, i: (bi, i, 0)
    mod_map = (lambda bi, i: (bi, 0, 0)) if gate.shape[0] > 1 else (lambda bi, i: (0, 0, 0))
    const2 = lambda bi, i: (0, 0)
    return pl.pallas_call(
        _out_proj_kernel,
        grid=(b, n // tm),
        in_specs=[pl.BlockSpec((1, tm, d), tok),
                  pl.BlockSpec((1, 1, d), mod_map),
                  pl.BlockSpec((1, tm, F_WIDTH), tok),
                  pl.BlockSpec((1, tm, F_WIDTH), tok),
                  pl.BlockSpec((1, tm, G_WIDTH), tok),
                  pl.BlockSpec((1, tm, A_WIDTH), tok),
                  pl.BlockSpec((1, tm, A_WIDTH), tok),
                  pl.BlockSpec(w_fmix.shape, const2),
                  pl.BlockSpec(w_out.shape, const2)],
        out_specs=pl.BlockSpec((1, tm, d), tok),
        out_shape=jax.ShapeDtypeStruct((b, n, d), F32),
        compiler_params=pltpu.CompilerParams(
            dimension_semantics=("arbitrary", "arbitrary"), vmem_limit_bytes=VMEM_LIMIT_BYTES),
        name="out_proj",
    )(x, gate, f, sfg, gout, o, sag, w_fmix, w_out)


def _rot_perm_sign():
    q = A_ROPE // 4
    src, sign = [], []
    for j in range(A_ROPE):
        blk, r = divmod(j, q)
        if blk % 2 == 0:
            src.append((blk + 1) * q + r)
            sign.append(-1.0)
        else:
            src.append((blk - 1) * q + r)
            sign.append(1.0)
    return jnp.array(src, jnp.int32), jnp.array(sign, F32)


def _dft_tables(n):
    idx = (jnp.arange(n, dtype=jnp.int32)[:, None] * jnp.arange(n, dtype=jnp.int32)[None, :]) % n
    ang = idx.astype(F32) * (2.0 * math.pi / n)
    return jnp.cos(ang), jnp.sin(ang)


def _rope_cos_sin(n):
    rows = n // GRID_W
    row = jnp.repeat(jnp.arange(rows, dtype=F32), GRID_W)
    col = jnp.tile(jnp.arange(GRID_W, dtype=F32), rows)
    half = A_ROPE // 2
    inv = ROPE_BASE ** (-jnp.arange(0, half, 2, dtype=F32) / half)
    ang_r = row[:, None] * inv[None, :]
    ang_c = col[:, None] * inv[None, :]
    ang = jnp.concatenate([ang_r, ang_r, ang_c, ang_c], axis=-1)
    return jnp.cos(ang), jnp.sin(ang)


def _head_tables(gain, cos, sin, src):
    n = cos.shape[0]
    g_nope = jnp.broadcast_to(gain[:A_NOPE][None, :], (n, A_NOPE))
    a = gain[A_NOPE:][None, :] * cos
    b = gain[A_NOPE:][src][None, :] * sin
    tab1 = jnp.concatenate([g_nope, a, b], axis=-1)
    tab2 = jnp.concatenate([jnp.zeros((n, A_NOPE), F32), b, a], axis=-1)
    return tab1, tab2


def _layer_weights(l, p, src, sign):
    w_in = p["w_in"][l]
    d = w_in.shape[0]
    sizes = (F_WIDTH, F_WIDTH, G_WIDTH, G_WIDTH, G_WIDTH, Q_LORA, KV_LORA, A_ROPE, A_WIDTH)
    parts, start = [], 0
    for s in sizes:
        parts.append(w_in[:, start:start + s])
        start += s
    w_kr = parts[7]
    w_rot = w_kr[:, src] * sign[None, :]
    kr_tile = jnp.concatenate([jnp.zeros((d, A_NOPE), F32), w_kr, w_rot], axis=-1)
    kr2_tile = jnp.concatenate([jnp.zeros((d, A_NOPE), F32), w_rot, w_kr], axis=-1)
    w_in_p = jnp.concatenate(parts[:7] + [kr_tile, kr2_tile, parts[8]], axis=-1).astype(BF16)

    w_uq = p["w_uq"][l].reshape(Q_LORA, A_HEADS, A_QK)
    w_uq_rope = w_uq[:, :, A_NOPE:]
    w_uq_p = jnp.concatenate([w_uq, w_uq_rope[:, :, src] * sign[None, None, :]], axis=-1)
    w_uq_t = w_uq_p.reshape(Q_LORA, A_HEADS * LANES).T.astype(BF16)

    w_ukv = p["w_ukv"][l].reshape(KV_LORA, A_HEADS, A_NOPE + A_V)
    w_uk_p = jnp.concatenate([w_ukv[:, :, :A_NOPE], jnp.zeros((KV_LORA, A_HEADS, LANES - A_NOPE), F32)], axis=-1)
    w_uk_p = w_uk_p.reshape(KV_LORA, A_HEADS * LANES).astype(BF16)
    w_uv_t = w_ukv[:, :, A_NOPE:].reshape(KV_LORA, A_WIDTH).T.astype(BF16)

    ws = p["g_ws"][l]
    ws_cat = jnp.transpose(ws, (1, 0, 2)).reshape(CHUNK, G_HEADS * CHUNK).astype(BF16)
    bs_tab = jnp.repeat(p["g_bs"][l].T, G_HEAD_DIM, axis=1)
    return {
        "norm_g": p["norm_g"][l][None, :], "w_in": w_in_p, "w_uq_t": w_uq_t, "w_uk": w_uk_p, "w_uv_t": w_uv_t,
        "g_ln_g": jnp.tile(p["g_ln_g"][l], G_HEADS)[None, :], "ws_cat": ws_cat, "bs_tab": bs_tab,
        "q_a_g": p["q_a_g"][l][None, :], "kv_a_g": p["kv_a_g"][l][None, :],
        "w_fmix": p["w_fmix"][l].astype(BF16), "w_out": p["w_out"][l].astype(BF16),
    }


def _token_tile(n):
    return 512 if n % 512 == 0 else 256


def kernel(x, c, ctx, c_ctx, w_mod, b_mod, norm_g, w_in, w_fmix, g_ln_g, g_ws, g_bs,
           q_a_g, w_uq, kv_a_g, w_ukv, q_norm_g, k_norm_g, w_out):
    p = dict(w_in=w_in, w_fmix=w_fmix, g_ln_g=g_ln_g, g_ws=g_ws, g_bs=g_bs, q_a_g=q_a_g, w_uq=w_uq,
             kv_a_g=kv_a_g, w_ukv=w_ukv, norm_g=norm_g, w_out=w_out)
    depth = w_mod.shape[0]
    b, n, d = x.shape
    n_ctx = ctx.shape[1]
    src, sign = _rot_perm_sign()

    rows = -(-(b + 1) // 8) * 8
    cc = jnp.concatenate([c, c_ctx[None, :], jnp.zeros((rows - b - 1, d), F32)], axis=0)
    mod = _modulation(_silu(cc), w_mod, b_mod)

    cos64, sin64 = _dft_tables(F_GROUP_DIM)
    eye_g = jnp.eye(F_GROUPS, dtype=F32)
    bc = jnp.kron(eye_g, cos64).astype(BF16)
    bs = jnp.kron(eye_g, sin64).astype(BF16)
    jm = jnp.kron(jnp.eye(G_HEADS, dtype=F32), jnp.full((G_HEAD_DIM, G_HEAD_DIM), 1.0 / G_HEAD_DIM, F32)).astype(BF16)
    cos_x, sin_x = _rope_cos_sin(n)
    cos_y, sin_y = jnp.ones((n_ctx, A_ROPE), F32), jnp.zeros((n_ctx, A_ROPE), F32)
    q_scale = (A_QK ** -0.5) * math.log2(math.e)

    y = ctx
    for l in range(depth):
        lw = _layer_weights(l, p, src, sign)
        lw.update(jm=jm)
        shift, scale, gate = (mod[l, :, i * d:(i + 1) * d] for i in range(3))
        tabs_x, tabs_y = {}, {}
        for tabs, cs, sn in ((tabs_x, cos_x, sin_x), (tabs_y, cos_y, sin_y)):
            tabs["qt"] = (_head_tables(q_norm_g[l], cs, sn, src)[0] * q_scale).T
            tabs["k1"], tabs["k2"] = _head_tables(k_norm_g[l], cs, sn, src)

        score_bound = (1.02 * A_QK * q_scale) * jnp.max(jnp.abs(q_norm_g[l])) * jnp.max(jnp.abs(k_norm_g[l]))

        tm_y, tm_x = _token_tile(n_ctx), _token_tile(n)
        n_keys = n + n_ctx
        fin_y, sfg_y, gout_y, q_y, k_all, vt_all, sag_y = _in_proj(
            y, scale[b:b + 1, None, :], shift[b:b + 1, None, :], lw, tabs_y, tm_y, n_keys, n)
        fin_x, sfg_x, gout_x, q_x, k_all, vt_all, sag_x = _in_proj(
            x, scale[:b, None, :], shift[:b, None, :], lw, tabs_x, tm_x, n_keys, 0, (k_all, vt_all))

        f_x = _pos_dft(fin_x, bc, bs)
        o_x = _attention(q_x, k_all, vt_all, 0, n_keys, tm_x, score_bound)
        x = _out_proj(x, gate[:b, None, :], f_x, sfg_x, gout_x, o_x, sag_x, lw["w_fmix"], lw["w_out"], tm_x)
        if l < depth - 1:
            f_y = _pos_dft(fin_y, bc, bs)
            o_y = _attention(q_y, k_all, vt_all, n, n_ctx, tm_y, score_bound)
            y = _out_proj(y, gate[b:b + 1, None, :], f_y, sfg_y, gout_y, o_y, sag_y,
                          lw["w_fmix"], lw["w_out"], tm_y)
    return x
```

```python
import functools
import math

import jax
import jax.numpy as jnp
from jax import lax
from jax.experimental import pallas as pl
from jax.experimental.pallas import tpu as pltpu

F32 = jnp.float32
BF16 = jnp.bfloat16

LANES = 128
VMEM_LIMIT_BYTES = 56 * 1024 * 1024

GRID_W = 64
F_GROUPS = 4
F_GROUP_DIM = 64
F_WIDTH = F_GROUPS * F_GROUP_DIM
G_HEADS = 4
G_HEAD_DIM = 64
G_WIDTH = G_HEADS * G_HEAD_DIM
CHUNK = 128
A_HEADS = 8
A_NOPE = 64
A_ROPE = 32
A_V = 64
A_QK = A_NOPE + A_ROPE
A_WIDTH = A_HEADS * A_V
Q_LORA = 256
KV_LORA = 128
ROPE_BASE = 10000.0
EPS = 1e-6
ONES_ROWS = 16
KEY_CHUNK = 256
MAX_SAFE_SHIFT = 60.0
HEAD_GROUP = 8
LOOKAHEAD = 2

OFF_FIN = 0
OFF_FGATE = OFF_FIN + F_WIDTH
OFF_GU = OFF_FGATE + F_WIDTH
OFF_GV = OFF_GU + G_WIDTH
OFF_GGATE = OFF_GV + G_WIDTH
OFF_CQ = OFF_GGATE + G_WIDTH
OFF_CKV = OFF_CQ + Q_LORA
OFF_KR = OFF_CKV + KV_LORA
OFF_KR2 = OFF_KR + LANES
OFF_AGATE = OFF_KR2 + LANES
IN_P = OFF_AGATE + A_WIDTH


def _silu(t):
    return t * (1.0 / (1.0 + jnp.exp(-t)))


def _dot(a, b):
    return jnp.dot(a, b, preferred_element_type=F32)


def _dot_hilo(t, w):
    hi = t.astype(BF16)
    lo = (t - hi.astype(F32)).astype(BF16)
    return _dot(hi, w) + _dot(lo, w)


def _mod_kernel(cc_ref, w_ref, b_ref, o_ref):
    o_ref[0] = lax.dot_general(cc_ref[...], w_ref[0], (((1,), (0,)), ((), ())),
                               precision=lax.Precision.HIGHEST,
                               preferred_element_type=F32) + b_ref[0]


def _modulation(cc, w_mod, b_mod):
    depth, d, d3 = w_mod.shape
    rows = cc.shape[0]
    tn = 1024
    return pl.pallas_call(
        _mod_kernel,
        grid=(depth, d3 // tn),
        in_specs=[pl.BlockSpec((rows, d), lambda l, j: (0, 0)),
                  pl.BlockSpec((1, d, tn), lambda l, j: (l, 0, j)),
                  pl.BlockSpec((1, 1, tn), lambda l, j: (l, 0, j))],
        out_specs=pl.BlockSpec((1, rows, tn), lambda l, j: (l, 0, j)),
        out_shape=jax.ShapeDtypeStruct((depth, rows, d3), F32),
        compiler_params=pltpu.CompilerParams(vmem_limit_bytes=VMEM_LIMIT_BYTES),
        name="modulation",
    )(cc, w_mod, b_mod.reshape(depth, 1, d3))


def _in_proj_kernel(*refs, fused):
    if fused:
        x = _out_proj_tile(*refs[:9])
        refs = refs[9:]
    else:
        x = refs[0][0]
        refs = refs[1:]
    (sc_ref, sh_ref, ng_ref, win_ref, jm_ref, gln_ref, ws_ref, bsb_ref, qag_ref, wuqt_ref, kvag_ref,
     wuk_ref, wuvt_ref, tqt_ref, tk1_ref, tk2_ref) = refs[:16]
    outs = refs[16:]
    if fused:
        outs[0][0] = x
        outs = outs[1:]
    fin_ref, sfg_ref, gout_ref, q_ref, k_ref, vt_ref, sag_ref = outs
    tm = x.shape[0]
    xn = x * lax.rsqrt(jnp.mean(x * x, axis=-1, keepdims=True) + EPS) * ng_ref[...]
    hb = (xn * (1.0 + sc_ref[0]) + sh_ref[0]).astype(BF16)

    def proj(off, width):
        return _dot(hb, win_ref[:, off:off + width])

    g_v = proj(OFF_GV, G_WIDTH)
    c_q = proj(OFF_CQ, Q_LORA)
    c_kv = proj(OFF_CKV, KV_LORA)
    kr = proj(OFF_KR, LANES)
    kr2 = proj(OFF_KR2, LANES) * tk2_ref[...]
    fin_ref[0] = proj(OFF_FIN, F_WIDTH).astype(BF16)
    jm = jm_ref[...]
    dv = g_v - _dot_hilo(g_v, jm)
    sfg_ref[0] = _silu(proj(OFF_FGATE, F_WIDTH)).astype(BF16)
    var = _dot_hilo(dv * dv, jm)
    g_u = proj(OFF_GU, G_WIDTH)
    g_gate = proj(OFF_GGATE, G_WIDTH)

    cqn = c_q * lax.rsqrt(jnp.mean(c_q * c_q, axis=-1, keepdims=True) + EPS) * qag_ref[...]
    qt_all = _dot(wuqt_ref[...], cqn.T.astype(BF16))
    tqt = tqt_ref[...]
    for h in range(A_HEADS):
        t = qt_all[h * LANES:(h + 1) * LANES]
        ss = jnp.sum(t[0:A_QK] * t[0:A_QK], axis=0, keepdims=True)
        q_ref[0, h] = (t * lax.rsqrt(ss * (1.0 / A_QK) + EPS) * tqt).astype(BF16)

    lane = lax.broadcasted_iota(jnp.int32, (1, LANES), 1)
    lane_mask = (lane < A_QK).astype(F32)
    ckvn = c_kv * lax.rsqrt(jnp.mean(c_kv * c_kv, axis=-1, keepdims=True) + EPS) * kvag_ref[...]
    ckvn_b = ckvn.astype(BF16)
    tk1 = tk1_ref[...]
    for h in range(A_HEADS):
        t = _dot(ckvn_b, wuk_ref[:, h * LANES:(h + 1) * LANES]) + kr
        ss = jnp.sum(t * t * lane_mask, axis=-1, keepdims=True)
        k_ref[0, h] = ((t * tk1 + kr2) * lax.rsqrt(ss * (1.0 / A_QK) + EPS)).astype(BF16)
    vt = _dot(wuvt_ref[...], ckvn.T.astype(BF16))
    vt_ref[0, :, 0:A_V, :] = vt.astype(BF16).reshape(A_HEADS, A_V, tm)
    vt_ref[0, :, A_V:, :] = jnp.ones((A_HEADS, ONES_ROWS, tm), BF16)

    sag_ref[0] = _silu(proj(OFF_AGATE, A_WIDTH)).astype(BF16)

    vln = dv * lax.rsqrt(var + EPS) * gln_ref[...]
    head_of_lane = lax.broadcasted_iota(jnp.int32, (CHUNK, G_WIDTH), 1) // G_HEAD_DIM
    for c in range(tm // CHUNK):
        rows = slice(c * CHUNK, (c + 1) * CHUNK)
        vc = vln[rows]
        stacked = jnp.concatenate(
            [jnp.where(head_of_lane == h, vc, 0.0).astype(BF16) for h in range(G_HEADS)], axis=0)
        mixed = _dot(ws_ref[...], stacked) + bsb_ref[...]
        gout_ref[0, rows, :] = (g_u[rows] * mixed * _silu(g_gate[rows])).astype(BF16)


def _in_proj(x, scale, shift, lw, tabs, tm, prev=None):
    b, n, d = x.shape
    mod_map = (lambda bi, i: (bi, 0, 0)) if scale.shape[0] > 1 else (lambda bi, i: (0, 0, 0))
    const2 = lambda bi, i: (0, 0)
    row_tab = lambda bi, i: (i, 0)
    tok = lambda bi, i: (bi, i, 0)
    full = lambda a: pl.BlockSpec(a.shape, const2)
    if prev is None:
        args, in_specs = [x], [pl.BlockSpec((1, tm, d), tok)]
    else:
        args, in_specs = _out_proj_operands(x, *prev, tm)
    weights = [lw["norm_g"], lw["w_in"], lw["jm"], lw["g_ln_g"], lw["ws_cat"], lw["bs_tab"], lw["q_a_g"],
               lw["w_uq_t"], lw["kv_a_g"], lw["w_uk"], lw["w_uv_t"]]
    args += [scale, shift] + weights + [tabs["qt"], tabs["k1"], tabs["k2"]]
    in_specs += [pl.BlockSpec((1, 1, d), mod_map), pl.BlockSpec((1, 1, d), mod_map)]
    in_specs += [full(a) for a in weights]
    in_specs += [pl.BlockSpec((LANES, tm), lambda bi, i: (0, i)),
                 pl.BlockSpec((tm, LANES), row_tab), pl.BlockSpec((tm, LANES), row_tab)]
    out_shape = [
        jax.ShapeDtypeStruct((b, n, F_WIDTH), BF16),
        jax.ShapeDtypeStruct((b, n, F_WIDTH), BF16),
        jax.ShapeDtypeStruct((b, n, G_WIDTH), BF16),
        jax.ShapeDtypeStruct((b, A_HEADS, LANES, n), BF16),
        jax.ShapeDtypeStruct((b, A_HEADS, n, LANES), BF16),
        jax.ShapeDtypeStruct((b, A_HEADS, A_V + ONES_ROWS, n), BF16),
        jax.ShapeDtypeStruct((b, n, A_WIDTH), BF16),
    ]
    out_specs = [
        pl.BlockSpec((1, tm, F_WIDTH), tok),
        pl.BlockSpec((1, tm, F_WIDTH), tok),
        pl.BlockSpec((1, tm, G_WIDTH), tok),
        pl.BlockSpec((1, A_HEADS, LANES, tm), lambda bi, i: (bi, 0, 0, i)),
        pl.BlockSpec((1, A_HEADS, tm, LANES), lambda bi, i: (bi, 0, i, 0)),
        pl.BlockSpec((1, A_HEADS, A_V + ONES_ROWS, tm), lambda bi, i: (bi, 0, 0, i)),
        pl.BlockSpec((1, tm, A_WIDTH), tok),
    ]
    if prev is not None:
        out_shape = [jax.ShapeDtypeStruct((b, n, d), F32)] + out_shape
        out_specs = [pl.BlockSpec((1, tm, d), tok)] + out_specs
    return pl.pallas_call(
        functools.partial(_in_proj_kernel, fused=prev is not None),
        grid=(b, n // tm),
        in_specs=in_specs,
        out_specs=out_specs,
        out_shape=out_shape,
        compiler_params=pltpu.CompilerParams(
            dimension_semantics=("arbitrary", "arbitrary"), vmem_limit_bytes=VMEM_LIMIT_BYTES),
        name="in_proj_fused" if prev is not None else "in_proj",
    )(*args)


DFT_T = 16


def _dft_stage1_kernel(wc_ref, ws_ref, u_ref, ct_ref, st_ref, zr_ref, zi_ref):
    r, t, w = u_ref.shape[1:]
    u = u_ref[0].reshape(r * t, w)
    yr = _dot(wc_ref[...], u)
    yi = _dot(ws_ref[...], u)
    ct, st = ct_ref[...].reshape(r * t, w), st_ref[...].reshape(r * t, w)
    zr_ref[0] = (yr * ct + yi * st).astype(BF16).reshape(r, t, w)
    zi_ref[0] = (yi * ct - yr * st).astype(BF16).reshape(r, t, w)


def _dft_stage2_kernel(zr_ref, zi_ref, d_ref, bc_ref, bs_ref, o_ref, *, norm):
    groups = o_ref.shape[2]
    xs = []
    for j in range(groups):
        rows = slice(j * GRID_W, (j + 1) * GRID_W)
        z = jnp.concatenate([zr_ref[0, rows, :], zi_ref[0, rows, :]], axis=0)
        xs.append(_dot(d_ref[...], z))
    xr = jnp.concatenate([x[0:GRID_W] for x in xs], axis=0).astype(BF16)
    xi = jnp.concatenate([x[GRID_W:] for x in xs], axis=0).astype(BF16)
    f = (_dot(xr, bc_ref[...]) + _dot(xi, bs_ref[...])) * norm
    for j in range(groups):
        o_ref[0, :, j, :] = f[j * GRID_W:(j + 1) * GRID_W]


def _pos_dft_factored(u, bc, bs):
    b, n, w = u.shape
    r = n // GRID_W
    cos_r, sin_r = _dft_tables(r)
    eye_t = jnp.eye(DFT_T, dtype=F32)
    wc = jnp.kron(cos_r, eye_t).astype(BF16)
    ws = jnp.kron(-sin_r, eye_t).astype(BF16)
    ang = (jnp.arange(r, dtype=jnp.int32)[:, None] * jnp.arange(GRID_W, dtype=jnp.int32)[None, :]
           ).astype(F32) * (2.0 * math.pi / n)
    ct = jnp.broadcast_to(jnp.cos(ang)[:, :, None], (r, GRID_W, w))
    st = jnp.broadcast_to(jnp.sin(ang)[:, :, None], (r, GRID_W, w))
    blk = lambda j, bi: (bi, 0, j, 0)
    zr, zi = pl.pallas_call(
        _dft_stage1_kernel,
        grid=(GRID_W // DFT_T, b),
        in_specs=[pl.BlockSpec(wc.shape, lambda j, bi: (0, 0)),
                  pl.BlockSpec(ws.shape, lambda j, bi: (0, 0)),
                  pl.BlockSpec((1, r, DFT_T, w), blk),
                  pl.BlockSpec((r, DFT_T, w), lambda j, bi: (0, j, 0)),
                  pl.BlockSpec((r, DFT_T, w), lambda j, bi: (0, j, 0))],
        out_specs=[pl.BlockSpec((1, r, DFT_T, w), blk), pl.BlockSpec((1, r, DFT_T, w), blk)],
        out_shape=[jax.ShapeDtypeStruct((b, r, GRID_W, w), BF16)] * 2,
        compiler_params=pltpu.CompilerParams(
            dimension_semantics=("arbitrary", "arbitrary"), vmem_limit_bytes=VMEM_LIMIT_BYTES),
        name="dft_stage1",
    )(wc, ws, u.reshape(b, r, GRID_W, w), ct, st)

    cos64, sin64 = _dft_tables(GRID_W)
    d = jnp.concatenate([jnp.concatenate([cos64, sin64], axis=1),
                         jnp.concatenate([-sin64, cos64], axis=1)], axis=0).astype(BF16)
    groups = min(32, r)
    f = pl.pallas_call(
        functools.partial(_dft_stage2_kernel, norm=1.0 / math.sqrt(n * F_GROUP_DIM)),
        grid=(b, r // groups),
        in_specs=[pl.BlockSpec((1, groups * GRID_W, w), lambda bi, g: (bi, g, 0)),
                  pl.BlockSpec((1, groups * GRID_W, w), lambda bi, g: (bi, g, 0)),
                  pl.BlockSpec(d.shape, lambda bi, g: (0, 0)),
                  pl.BlockSpec(bc.shape, lambda bi, g: (0, 0)),
                  pl.BlockSpec(bs.shape, lambda bi, g: (0, 0))],
        out_specs=pl.BlockSpec((1, GRID_W, groups, w), lambda bi, g: (bi, 0, g, 0)),
        out_shape=jax.ShapeDtypeStruct((b, GRID_W, r, w), F32),
        compiler_params=pltpu.CompilerParams(
            dimension_semantics=("arbitrary", "arbitrary"), vmem_limit_bytes=VMEM_LIMIT_BYTES),
        name="dft_stage2",
    )(zr.reshape(b, n, w), zi.reshape(b, n, w), d, bc, bs)
    return f.reshape(b, n, w)


def _dft_dense_kernel(cn_ref, sn_ref, u_ref, bc_ref, bs_ref, o_ref, *, norm):
    u = u_ref[0]
    pc = _dot(cn_ref[...], u).astype(BF16)
    ps = _dot(sn_ref[...], u).astype(BF16)
    o_ref[0] = (_dot(pc, bc_ref[...]) - _dot(ps, bs_ref[...])) * norm


def _pos_dft_dense(u, bc, bs):
    b, n, w = u.shape
    cos_n, sin_n = _dft_tables(n)
    const2 = lambda bi: (0, 0)
    return pl.pallas_call(
        functools.partial(_dft_dense_kernel, norm=1.0 / math.sqrt(n * F_GROUP_DIM)),
        grid=(b,),
        in_specs=[pl.BlockSpec((n, n), const2), pl.BlockSpec((n, n), const2),
                  pl.BlockSpec((1, n, w), lambda bi: (bi, 0, 0)),
                  pl.BlockSpec(bc.shape, const2), pl.BlockSpec(bs.shape, const2)],
        out_specs=pl.BlockSpec((1, n, w), lambda bi: (bi, 0, 0)),
        out_shape=jax.ShapeDtypeStruct((b, n, w), F32),
        compiler_params=pltpu.CompilerParams(
            dimension_semantics=("arbitrary",), vmem_limit_bytes=VMEM_LIMIT_BYTES),
        name="dft_dense",
    )(cos_n.astype(BF16), sin_n.astype(BF16), u, bc, bs)


def _pos_dft(u, bc, bs):
    r = u.shape[1] // GRID_W
    if r % 16 == 0:
        return _pos_dft_factored(u, bc, bs)
    return _pos_dft_dense(u, bc, bs)


def _key_chunks(k_refs):
    chunks = []
    for seg, k_ref in enumerate(k_refs):
        for c in range(k_ref.shape[2] // KEY_CHUNK):
            chunks.append((seg, c, len(chunks)))
    return chunks


def _rows(c):
    return slice(c * KEY_CHUNK, (c + 1) * KEY_CHUNK)


def _attention_kernel(*refs, nseg):
    qt_ref, k_refs, vt_refs = refs[0], refs[1:1 + nseg], refs[1 + nseg:1 + 2 * nseg]
    o_ref, s0_ref, s1_ref, ot_ref = refs[1 + 2 * nseg:]
    heads, tq = qt_ref.shape[1], qt_ref.shape[3]
    chunks = _key_chunks(k_refs)
    s_bufs = (s0_ref, s1_ref)

    def phase(h_prod, h_cons, m_cons, prod_buf):
        m_part, acc = None, None
        for seg, c, row in chunks:
            if h_prod is not None:
                s_t = _dot(k_refs[seg][0, h_prod, _rows(c), :], qt_ref[0, h_prod])
                s_bufs[prod_buf][_rows(row), :] = s_t
                pm = jnp.max(s_t.reshape(KEY_CHUNK // 8, 8, tq), axis=0)
                m_part = pm if m_part is None else jnp.maximum(m_part, pm)
            if h_cons is not None:
                p_t = jnp.exp2(s_bufs[1 - prod_buf][_rows(row), :] - m_cons).astype(BF16)
                d = _dot(vt_refs[seg][0, h_cons, :, _rows(c)], p_t)
                acc = d if acc is None else acc + d
        if h_cons is not None:
            ot_ref[pl.ds(pl.multiple_of(h_cons * A_V, A_V), A_V), :] = acc[0:A_V] / acc[A_V:A_V + 1]
        return None if h_prod is None else jnp.max(m_part, axis=0, keepdims=True)

    def head_pair(j, m):
        m = phase(2 * j + 1, 2 * j, m, 1)
        return phase(2 * j + 2, 2 * j + 1, m, 0)

    m = phase(0, None, None, 0)
    m = lax.fori_loop(0, (heads - 2) // 2, head_pair, m)
    m = phase(heads - 1, heads - 2, m, 1)
    phase(None, heads - 1, m, 0)
    o_ref[0] = ot_ref[...].T.astype(BF16)


def _attention_bounded_kernel(*refs, nseg):
    shift_ref, qt_ref, k_refs, vt_refs = refs[0], refs[1], refs[2:2 + nseg], refs[2 + nseg:2 + 2 * nseg]
    o_ref, ot_ref = refs[2 + 2 * nseg:]
    heads = qt_ref.shape[1]
    chunks = _key_chunks(k_refs)
    shift = shift_ref[0, 0]

    def head_group(j, carry):
        items = [(j * HEAD_GROUP + g, chunk) for g in range(HEAD_GROUP) for chunk in chunks]
        scores, acc = [], None
        for i in range(len(items) + LOOKAHEAD):
            if i < len(items):
                h, (seg, c, _) = items[i]
                scores.append(_dot(k_refs[seg][0, h, _rows(c), :], qt_ref[0, h]))
            if i >= LOOKAHEAD:
                h, (seg, c, row) = items[i - LOOKAHEAD]
                d = _dot(vt_refs[seg][0, h, :, _rows(c)], jnp.exp2(scores.pop(0) - shift).astype(BF16))
                acc = d if row == 0 else acc + d
                if row == len(chunks) - 1:
                    ot_ref[pl.ds(pl.multiple_of(h * A_V, A_V), A_V), :] = acc[0:A_V] / acc[A_V:A_V + 1]
        return carry

    lax.fori_loop(0, heads // HEAD_GROUP, head_group, 0)
    o_ref[0] = ot_ref[...].T.astype(BF16)


def _attention(qt, ks, vts, tq, score_bound):
    b, heads, _, n = qt.shape
    nseg = len(ks)
    n_keys = sum(k.shape[2] for k in ks)
    kvmap = lambda bi, qi: (bi, 0, 0, 0)
    q_spec = pl.BlockSpec((1, heads, LANES, tq), lambda bi, qi: (bi, 0, 0, qi))
    o_spec = pl.BlockSpec((1, tq, heads * A_V), lambda bi, qi: (bi, qi, 0))
    o_shape = jax.ShapeDtypeStruct((b, n, heads * A_V), BF16)
    params = pltpu.CompilerParams(dimension_semantics=("arbitrary", "arbitrary"),
                                  vmem_limit_bytes=VMEM_LIMIT_BYTES)

    def kv_specs(**kw):
        return ([pl.BlockSpec((1,) + k.shape[1:], kvmap, **kw) for k in ks]
                + [pl.BlockSpec((1,) + vt.shape[1:], kvmap, **kw) for vt in vts])

    def bounded(qt, *kv):
        return pl.pallas_call(
            functools.partial(_attention_bounded_kernel, nseg=nseg),
            grid=(b, n // tq),
            in_specs=[pl.BlockSpec(memory_space=pltpu.SMEM), q_spec] + kv_specs(),
            out_specs=o_spec,
            out_shape=o_shape,
            scratch_shapes=[pltpu.VMEM((heads * A_V, tq), F32)],
            compiler_params=params,
            name="attention_bounded",
        )(jnp.reshape(score_bound, (1, 1)), qt, *kv)

    def row_max(qt, *kv):
        return pl.pallas_call(
            functools.partial(_attention_kernel, nseg=nseg),
            grid=(b, n // tq),
            in_specs=[q_spec] + kv_specs(pipeline_mode=pl.Buffered(1)),
            out_specs=o_spec,
            out_shape=o_shape,
            scratch_shapes=[pltpu.VMEM((n_keys, tq), F32), pltpu.VMEM((n_keys, tq), F32),
                            pltpu.VMEM((heads * A_V, tq), F32)],
            compiler_params=params,
            name="attention",
        )(qt, *kv)

    return lax.cond(score_bound <= MAX_SAFE_SHIFT, bounded, row_max, qt, *ks, *vts)


def _out_proj_tile(x_ref, gate_ref, f_ref, sfg_ref, gout_ref, o_ref, sag_ref, wf_ref, wo_ref):
    f_out = (_dot(f_ref[0].astype(BF16), wf_ref[...]) * sfg_ref[0].astype(F32)).astype(BF16)
    o_g = (o_ref[0].astype(F32) * sag_ref[0].astype(F32)).astype(BF16)
    y = (_dot(f_out, wo_ref[0:F_WIDTH, :])
         + _dot(gout_ref[0], wo_ref[F_WIDTH:F_WIDTH + G_WIDTH, :])
         + _dot(o_g, wo_ref[F_WIDTH + G_WIDTH:, :]))
    return x_ref[0] + gate_ref[0] * y


def _out_proj_kernel(*refs):
    refs[-1][0] = _out_proj_tile(*refs[:-1])


def _out_proj_operands(x, gate, f, sfg, gout, o, sag, w_fmix, w_out, tm):
    d = x.shape[2]
    tok = lambda bi, i: (bi, i, 0)
    mod_map = (lambda bi, i: (bi, 0, 0)) if gate.shape[0] > 1 else (lambda bi, i: (0, 0, 0))
    const2 = lambda bi, i: (0, 0)
    in_specs = [pl.BlockSpec((1, tm, d), tok),
                pl.BlockSpec((1, 1, d), mod_map),
                pl.BlockSpec((1, tm, F_WIDTH), tok),
                pl.BlockSpec((1, tm, F_WIDTH), tok),
                pl.BlockSpec((1, tm, G_WIDTH), tok),
                pl.BlockSpec((1, tm, A_WIDTH), tok),
                pl.BlockSpec((1, tm, A_WIDTH), tok),
                pl.BlockSpec(w_fmix.shape, const2),
                pl.BlockSpec(w_out.shape, const2)]
    return [x, gate, f, sfg, gout, o, sag, w_fmix, w_out], in_specs


def _out_proj(x, gate, f, sfg, gout, o, sag, w_fmix, w_out, tm):
    b, n, d = x.shape
    args, in_specs = _out_proj_operands(x, gate, f, sfg, gout, o, sag, w_fmix, w_out, tm)
    return pl.pallas_call(
        _out_proj_kernel,
        grid=(b, n // tm),
        in_specs=in_specs,
        out_specs=pl.BlockSpec((1, tm, d), lambda bi, i: (bi, i, 0)),
        out_shape=jax.ShapeDtypeStruct((b, n, d), F32),
        compiler_params=pltpu.CompilerParams(
            dimension_semantics=("arbitrary", "arbitrary"), vmem_limit_bytes=VMEM_LIMIT_BYTES),
        name="out_proj",
    )(*args)


def _rot_perm_sign():
    q = A_ROPE // 4
    src, sign = [], []
    for j in range(A_ROPE):
        blk, r = divmod(j, q)
        if blk % 2 == 0:
            src.append((blk + 1) * q + r)
            sign.append(-1.0)
        else:
            src.append((blk - 1) * q + r)
            sign.append(1.0)
    return jnp.array(src, jnp.int32), jnp.array(sign, F32)


def _dft_tables(n):
    idx = (jnp.arange(n, dtype=jnp.int32)[:, None] * jnp.arange(n, dtype=jnp.int32)[None, :]) % n
    ang = idx.astype(F32) * (2.0 * math.pi / n)
    return jnp.cos(ang), jnp.sin(ang)


def _rope_cos_sin(n):
    rows = n // GRID_W
    row = jnp.repeat(jnp.arange(rows, dtype=F32), GRID_W)
    col = jnp.tile(jnp.arange(GRID_W, dtype=F32), rows)
    half = A_ROPE // 2
    inv = ROPE_BASE ** (-jnp.arange(0, half, 2, dtype=F32) / half)
    ang_r = row[:, None] * inv[None, :]
    ang_c = col[:, None] * inv[None, :]
    ang = jnp.concatenate([ang_r, ang_r, ang_c, ang_c], axis=-1)
    return jnp.cos(ang), jnp.sin(ang)


def _head_tables(gain, cos, sin, src):
    n = cos.shape[0]
    g_nope = jnp.broadcast_to(gain[:A_NOPE][None, :], (n, A_NOPE))
    a = gain[A_NOPE:][None, :] * cos
    b = gain[A_NOPE:][src][None, :] * sin
    tab1 = jnp.concatenate([g_nope, a, b], axis=-1)
    tab2 = jnp.concatenate([jnp.zeros((n, A_NOPE), F32), b, a], axis=-1)
    return tab1, tab2


def _layer_weights(l, p, src, sign):
    w_in = p["w_in"][l]
    d = w_in.shape[0]
    sizes = (F_WIDTH, F_WIDTH, G_WIDTH, G_WIDTH, G_WIDTH, Q_LORA, KV_LORA, A_ROPE, A_WIDTH)
    parts, start = [], 0
    for s in sizes:
        parts.append(w_in[:, start:start + s])
        start += s
    w_kr = parts[7]
    w_rot = w_kr[:, src] * sign[None, :]
    kr_tile = jnp.concatenate([jnp.zeros((d, A_NOPE), F32), w_kr, w_rot], axis=-1)
    kr2_tile = jnp.concatenate([jnp.zeros((d, A_NOPE), F32), w_rot, w_kr], axis=-1)
    w_in_p = jnp.concatenate(parts[:7] + [kr_tile, kr2_tile, parts[8]], axis=-1).astype(BF16)

    w_uq = p["w_uq"][l].reshape(Q_LORA, A_HEADS, A_QK)
    w_uq_rope = w_uq[:, :, A_NOPE:]
    w_uq_p = jnp.concatenate([w_uq, w_uq_rope[:, :, src] * sign[None, None, :]], axis=-1)
    w_uq_t = w_uq_p.reshape(Q_LORA, A_HEADS * LANES).T.astype(BF16)

    w_ukv = p["w_ukv"][l].reshape(KV_LORA, A_HEADS, A_NOPE + A_V)
    w_uk_p = jnp.concatenate([w_ukv[:, :, :A_NOPE], jnp.zeros((KV_LORA, A_HEADS, LANES - A_NOPE), F32)], axis=-1)
    w_uk_p = w_uk_p.reshape(KV_LORA, A_HEADS * LANES).astype(BF16)
    w_uv_t = w_ukv[:, :, A_NOPE:].reshape(KV_LORA, A_WIDTH).T.astype(BF16)

    ws = p["g_ws"][l]
    ws_cat = jnp.transpose(ws, (1, 0, 2)).reshape(CHUNK, G_HEADS * CHUNK).astype(BF16)
    bs_tab = jnp.repeat(p["g_bs"][l].T, G_HEAD_DIM, axis=1)
    return {
        "norm_g": p["norm_g"][l][None, :], "w_in": w_in_p, "w_uq_t": w_uq_t, "w_uk": w_uk_p, "w_uv_t": w_uv_t,
        "g_ln_g": jnp.tile(p["g_ln_g"][l], G_HEADS)[None, :], "ws_cat": ws_cat, "bs_tab": bs_tab,
        "q_a_g": p["q_a_g"][l][None, :], "kv_a_g": p["kv_a_g"][l][None, :],
        "w_fmix": p["w_fmix"][l].astype(BF16), "w_out": p["w_out"][l].astype(BF16),
    }


def _token_tile(n):
    return 512 if n % 512 == 0 else 256


def kernel(x, c, ctx, c_ctx, w_mod, b_mod, norm_g, w_in, w_fmix, g_ln_g, g_ws, g_bs,
           q_a_g, w_uq, kv_a_g, w_ukv, q_norm_g, k_norm_g, w_out):
    p = dict(w_in=w_in, w_fmix=w_fmix, g_ln_g=g_ln_g, g_ws=g_ws, g_bs=g_bs, q_a_g=q_a_g, w_uq=w_uq,
             kv_a_g=kv_a_g, w_ukv=w_ukv, norm_g=norm_g, w_out=w_out)
    depth = w_mod.shape[0]
    b, n, d = x.shape
    n_ctx = ctx.shape[1]
    src, sign = _rot_perm_sign()

    rows = -(-(b + 1) // 8) * 8
    cc = jnp.concatenate([c, c_ctx[None, :], jnp.zeros((rows - b - 1, d), F32)], axis=0)
    mod = _modulation(_silu(cc), w_mod, b_mod)

    cos64, sin64 = _dft_tables(F_GROUP_DIM)
    eye_g = jnp.eye(F_GROUPS, dtype=F32)
    bc = jnp.kron(eye_g, cos64).astype(BF16)
    bs = jnp.kron(eye_g, sin64).astype(BF16)
    jm = jnp.kron(jnp.eye(G_HEADS, dtype=F32), jnp.full((G_HEAD_DIM, G_HEAD_DIM), 1.0 / G_HEAD_DIM, F32)).astype(BF16)
    cos_x, sin_x = _rope_cos_sin(n)
    cos_y, sin_y = jnp.ones((n_ctx, A_ROPE), F32), jnp.zeros((n_ctx, A_ROPE), F32)
    q_scale = (A_QK ** -0.5) * math.log2(math.e)

    y, prev_x, prev_y = ctx, None, None
    for l in range(depth):
        lw = _layer_weights(l, p, src, sign)
        lw.update(jm=jm)
        shift, scale, gate = (mod[l, :, i * d:(i + 1) * d] for i in range(3))
        tabs_x, tabs_y = {}, {}
        for tabs, cs, sn in ((tabs_x, cos_x, sin_x), (tabs_y, cos_y, sin_y)):
            tabs["qt"] = (_head_tables(q_norm_g[l], cs, sn, src)[0] * q_scale).T
            tabs["k1"], tabs["k2"] = _head_tables(k_norm_g[l], cs, sn, src)

        score_bound = (1.02 * A_QK * q_scale) * jnp.max(jnp.abs(q_norm_g[l])) * jnp.max(jnp.abs(k_norm_g[l]))

        tm_y, tm_x = _token_tile(n_ctx), _token_tile(n)
        outs_y = _in_proj(y, scale[b:b + 1, None, :], shift[b:b + 1, None, :], lw, tabs_y, tm_y, prev_y)
        outs_x = _in_proj(x, scale[:b, None, :], shift[:b, None, :], lw, tabs_x, tm_x, prev_x)
        if l > 0:
            y, x = outs_y[0], outs_x[0]
            outs_y, outs_x = outs_y[1:], outs_x[1:]
        fin_y, sfg_y, gout_y, q_y, k_y, vt_y, sag_y = outs_y
        fin_x, sfg_x, gout_x, q_x, k_x, vt_x, sag_x = outs_x

        f_x = _pos_dft(fin_x, bc, bs)
        o_x = _attention(q_x, [k_x, k_y], [vt_x, vt_y], tm_x, score_bound)
        prev_x = (gate[:b, None, :], f_x, sfg_x, gout_x, o_x, sag_x, lw["w_fmix"], lw["w_out"])
        if l < depth - 1:
            f_y = _pos_dft(fin_y, bc, bs)
            o_y = _attention(q_y, [k_y], [vt_y], tm_y, score_bound)
            prev_y = (gate[b:b + 1, None, :], f_y, sfg_y, gout_y, o_y, sag_y, lw["w_fmix"], lw["w_out"])
    return _out_proj(x, *prev_x, _token_tile(n))
```

```python
import functools
import math

import jax
import jax.numpy as jnp
from jax import lax
from jax.experimental import pallas as pl
from jax.experimental.pallas import tpu as pltpu

F32 = jnp.float32
BF16 = jnp.bfloat16

LANES = 128
VMEM_LIMIT_BYTES = 56 * 1024 * 1024

GRID_W = 64
F_GROUPS = 4
F_GROUP_DIM = 64
F_WIDTH = F_GROUPS * F_GROUP_DIM
G_HEADS = 4
G_HEAD_DIM = 64
G_WIDTH = G_HEADS * G_HEAD_DIM
CHUNK = 128
A_HEADS = 8
A_NOPE = 64
A_ROPE = 32
A_V = 64
A_QK = A_NOPE + A_ROPE
A_WIDTH = A_HEADS * A_V
Q_LORA = 256
KV_LORA = 128
ROPE_BASE = 10000.0
EPS = 1e-6
KEY_CHUNK = 256
MAX_SAFE_SHIFT = 60.0
HEAD_GROUP = 8
LOOKAHEAD = 2

OFF_FIN = 0
OFF_FGATE = OFF_FIN + F_WIDTH
OFF_GU = OFF_FGATE + F_WIDTH
OFF_GV = OFF_GU + G_WIDTH
OFF_GGATE = OFF_GV + G_WIDTH
OFF_CQ = OFF_GGATE + G_WIDTH
OFF_CKV = OFF_CQ + Q_LORA
OFF_KR = OFF_CKV + KV_LORA
OFF_KR2 = OFF_KR + LANES
OFF_AGATE = OFF_KR2 + LANES
IN_P = OFF_AGATE + A_WIDTH


def _silu(t):
    return t * (1.0 / (1.0 + jnp.exp(-t)))


def _dot(a, b):
    return jnp.dot(a, b, preferred_element_type=F32)


def _dot_hilo(t, w):
    hi = t.astype(BF16)
    lo = (t - hi.astype(F32)).astype(BF16)
    return _dot(hi, w) + _dot(lo, w)


def _mod_kernel(cc_ref, w_ref, b_ref, o_ref):
    o_ref[0] = lax.dot_general(cc_ref[...], w_ref[0], (((1,), (0,)), ((), ())),
                               precision=lax.Precision.HIGHEST,
                               preferred_element_type=F32) + b_ref[0]


def _modulation(cc, w_mod, b_mod):
    depth, d, d3 = w_mod.shape
    rows = cc.shape[0]
    tn = 1024
    return pl.pallas_call(
        _mod_kernel,
        grid=(depth, d3 // tn),
        in_specs=[pl.BlockSpec((rows, d), lambda l, j: (0, 0)),
                  pl.BlockSpec((1, d, tn), lambda l, j: (l, 0, j)),
                  pl.BlockSpec((1, 1, tn), lambda l, j: (l, 0, j))],
        out_specs=pl.BlockSpec((1, rows, tn), lambda l, j: (l, 0, j)),
        out_shape=jax.ShapeDtypeStruct((depth, rows, d3), F32),
        compiler_params=pltpu.CompilerParams(vmem_limit_bytes=VMEM_LIMIT_BYTES),
        name="modulation",
    )(cc, w_mod, b_mod.reshape(depth, 1, d3))


def _in_proj_kernel(*refs, fused):
    if fused:
        x = _out_proj_tile(*refs[:9])
        refs = refs[9:]
    else:
        x = refs[0][0]
        refs = refs[1:]
    (sc_ref, sh_ref, ng_ref, win_ref, jm_ref, gln_ref, ws_ref, bsb_ref, qag_ref, wuqt_ref, kvag_ref,
     wuk_ref, wuvt_ref, tqt_ref, tk1_ref, tk2_ref) = refs[:16]
    outs = refs[16:]
    if fused:
        outs[0][0] = x
        outs = outs[1:]
    fin_ref, sfg_ref, gout_ref, q_ref, k_ref, vt_ref, sag_ref = outs
    tm = x.shape[0]
    xn = x * lax.rsqrt(jnp.mean(x * x, axis=-1, keepdims=True) + EPS) * ng_ref[...]
    hb = (xn * (1.0 + sc_ref[0]) + sh_ref[0]).astype(BF16)

    def proj(off, width):
        return _dot(hb, win_ref[:, off:off + width])

    g_v = proj(OFF_GV, G_WIDTH)
    c_q = proj(OFF_CQ, Q_LORA)
    ckv_kr = proj(OFF_CKV, KV_LORA + LANES)
    c_kv, kr = ckv_kr[:, 0:KV_LORA], ckv_kr[:, KV_LORA:]
    kr2 = proj(OFF_KR2, LANES) * tk2_ref[...]
    fin_ref[0] = proj(OFF_FIN, F_WIDTH).astype(BF16)
    jm = jm_ref[...]
    dv = g_v - _dot_hilo(g_v, jm)
    sfg_ref[0] = _silu(proj(OFF_FGATE, F_WIDTH)).astype(BF16)
    var = _dot_hilo(dv * dv, jm)
    g_u = proj(OFF_GU, G_WIDTH)
    g_gate = proj(OFF_GGATE, G_WIDTH)

    cqn = c_q * lax.rsqrt(jnp.mean(c_q * c_q, axis=-1, keepdims=True) + EPS) * qag_ref[...]
    qt_all = _dot(wuqt_ref[...], cqn.T.astype(BF16))
    tqt = tqt_ref[...]
    for h in range(A_HEADS):
        t = qt_all[h * LANES:(h + 1) * LANES]
        ss = jnp.sum(t[0:A_QK] * t[0:A_QK], axis=0, keepdims=True)
        q_ref[0, h] = (t * lax.rsqrt(ss * (1.0 / A_QK) + EPS) * tqt).astype(BF16)

    lane = lax.broadcasted_iota(jnp.int32, (1, LANES), 1)
    lane_mask = (lane < A_QK).astype(F32)
    ckvn = c_kv * lax.rsqrt(jnp.mean(c_kv * c_kv, axis=-1, keepdims=True) + EPS) * kvag_ref[...]
    ckvn_b = ckvn.astype(BF16)
    tk1 = tk1_ref[...]
    for h in range(0, A_HEADS, 2):
        t_pair = _dot(ckvn_b, wuk_ref[:, h * LANES:(h + 2) * LANES])
        for g in range(2):
            t = t_pair[:, g * LANES:(g + 1) * LANES] + kr
            ss = jnp.sum(t * t * lane_mask, axis=-1, keepdims=True)
            k_ref[0, h + g] = ((t * tk1 + kr2) * lax.rsqrt(ss * (1.0 / A_QK) + EPS)).astype(BF16)
    vt = _dot(wuvt_ref[...], ckvn.T.astype(BF16))
    vt_ref[0] = vt.astype(BF16).reshape(A_HEADS, A_V, tm)

    sag_ref[0] = _silu(proj(OFF_AGATE, A_WIDTH)).astype(BF16)

    vln = dv * lax.rsqrt(var + EPS) * gln_ref[...]
    head_of_lane = lax.broadcasted_iota(jnp.int32, (CHUNK, G_WIDTH), 1) // G_HEAD_DIM
    for c in range(tm // CHUNK):
        rows = slice(c * CHUNK, (c + 1) * CHUNK)
        vc = vln[rows]
        stacked = jnp.concatenate(
            [jnp.where(head_of_lane == h, vc, 0.0).astype(BF16) for h in range(G_HEADS)], axis=0)
        mixed = _dot(ws_ref[...], stacked) + bsb_ref[...]
        gout_ref[0, rows, :] = (g_u[rows] * mixed * _silu(g_gate[rows])).astype(BF16)


def _in_proj(x, scale, shift, lw, tabs, tm, prev=None):
    b, n, d = x.shape
    mod_map = (lambda bi, i: (bi, 0, 0)) if scale.shape[0] > 1 else (lambda bi, i: (0, 0, 0))
    const2 = lambda bi, i: (0, 0)
    row_tab = lambda bi, i: (i, 0)
    tok = lambda bi, i: (bi, i, 0)
    full = lambda a: pl.BlockSpec(a.shape, const2)
    if prev is None:
        args, in_specs = [x], [pl.BlockSpec((1, tm, d), tok)]
    else:
        args, in_specs = _out_proj_operands(x, *prev, tm)
    weights = [lw["norm_g"], lw["w_in"], lw["jm"], lw["g_ln_g"], lw["ws_cat"], lw["bs_tab"], lw["q_a_g"],
               lw["w_uq_t"], lw["kv_a_g"], lw["w_uk"], lw["w_uv_t"]]
    args += [scale, shift] + weights + [tabs["qt"], tabs["k1"], tabs["k2"]]
    in_specs += [pl.BlockSpec((1, 1, d), mod_map), pl.BlockSpec((1, 1, d), mod_map)]
    in_specs += [full(a) for a in weights]
    in_specs += [pl.BlockSpec((LANES, tm), lambda bi, i: (0, i)),
                 pl.BlockSpec((tm, LANES), row_tab), pl.BlockSpec((tm, LANES), row_tab)]
    out_shape = [
        jax.ShapeDtypeStruct((b, n, F_WIDTH), BF16),
        jax.ShapeDtypeStruct((b, n, F_WIDTH), BF16),
        jax.ShapeDtypeStruct((b, n, G_WIDTH), BF16),
        jax.ShapeDtypeStruct((b, A_HEADS, LANES, n), BF16),
        jax.ShapeDtypeStruct((b, A_HEADS, n, LANES), BF16),
        jax.ShapeDtypeStruct((b, A_HEADS, A_V, n), BF16),
        jax.ShapeDtypeStruct((b, n, A_WIDTH), BF16),
    ]
    out_specs = [
        pl.BlockSpec((1, tm, F_WIDTH), tok),
        pl.BlockSpec((1, tm, F_WIDTH), tok),
        pl.BlockSpec((1, tm, G_WIDTH), tok),
        pl.BlockSpec((1, A_HEADS, LANES, tm), lambda bi, i: (bi, 0, 0, i)),
        pl.BlockSpec((1, A_HEADS, tm, LANES), lambda bi, i: (bi, 0, i, 0)),
        pl.BlockSpec((1, A_HEADS, A_V, tm), lambda bi, i: (bi, 0, 0, i)),
        pl.BlockSpec((1, tm, A_WIDTH), tok),
    ]
    if prev is not None:
        out_shape = [jax.ShapeDtypeStruct((b, n, d), F32)] + out_shape
        out_specs = [pl.BlockSpec((1, tm, d), tok)] + out_specs
    return pl.pallas_call(
        functools.partial(_in_proj_kernel, fused=prev is not None),
        grid=(b, n // tm),
        in_specs=in_specs,
        out_specs=out_specs,
        out_shape=out_shape,
        compiler_params=pltpu.CompilerParams(
            dimension_semantics=("arbitrary", "arbitrary"), vmem_limit_bytes=VMEM_LIMIT_BYTES),
        name="in_proj_fused" if prev is not None else "in_proj",
    )(*args)


DFT_T = 16


def _dft_stage1_kernel(wc_ref, ws_ref, u_ref, ct_ref, st_ref, zr_ref, zi_ref):
    r, t, w = u_ref.shape[1:]
    u = u_ref[0].reshape(r * t, w)
    yr = _dot(wc_ref[...], u)
    yi = _dot(ws_ref[...], u)
    ct, st = ct_ref[...].reshape(r * t, w), st_ref[...].reshape(r * t, w)
    zr_ref[0] = (yr * ct + yi * st).astype(BF16).reshape(r, t, w)
    zi_ref[0] = (yi * ct - yr * st).astype(BF16).reshape(r, t, w)


def _dft_stage2_kernel(zr_ref, zi_ref, d_ref, bc_ref, bs_ref, o_ref, *, norm):
    groups = o_ref.shape[2]
    xs = []
    for j in range(groups):
        rows = slice(j * GRID_W, (j + 1) * GRID_W)
        z = jnp.concatenate([zr_ref[0, rows, :], zi_ref[0, rows, :]], axis=0)
        xs.append(_dot(d_ref[...], z))
    xr = jnp.concatenate([x[0:GRID_W] for x in xs], axis=0).astype(BF16)
    xi = jnp.concatenate([x[GRID_W:] for x in xs], axis=0).astype(BF16)
    f = (_dot(xr, bc_ref[...]) + _dot(xi, bs_ref[...])) * norm
    for j in range(groups):
        o_ref[0, :, j, :] = f[j * GRID_W:(j + 1) * GRID_W]


def _pos_dft_factored(u, bc, bs):
    b, n, w = u.shape
    r = n // GRID_W
    cos_r, sin_r = _dft_tables(r)
    eye_t = jnp.eye(DFT_T, dtype=F32)
    wc = jnp.kron(cos_r, eye_t).astype(BF16)
    ws = jnp.kron(-sin_r, eye_t).astype(BF16)
    ang = (jnp.arange(r, dtype=jnp.int32)[:, None] * jnp.arange(GRID_W, dtype=jnp.int32)[None, :]
           ).astype(F32) * (2.0 * math.pi / n)
    ct = jnp.broadcast_to(jnp.cos(ang)[:, :, None], (r, GRID_W, w))
    st = jnp.broadcast_to(jnp.sin(ang)[:, :, None], (r, GRID_W, w))
    blk = lambda j, bi: (bi, 0, j, 0)
    zr, zi = pl.pallas_call(
        _dft_stage1_kernel,
        grid=(GRID_W // DFT_T, b),
        in_specs=[pl.BlockSpec(wc.shape, lambda j, bi: (0, 0)),
                  pl.BlockSpec(ws.shape, lambda j, bi: (0, 0)),
                  pl.BlockSpec((1, r, DFT_T, w), blk),
                  pl.BlockSpec((r, DFT_T, w), lambda j, bi: (0, j, 0)),
                  pl.BlockSpec((r, DFT_T, w), lambda j, bi: (0, j, 0))],
        out_specs=[pl.BlockSpec((1, r, DFT_T, w), blk), pl.BlockSpec((1, r, DFT_T, w), blk)],
        out_shape=[jax.ShapeDtypeStruct((b, r, GRID_W, w), BF16)] * 2,
        compiler_params=pltpu.CompilerParams(
            dimension_semantics=("arbitrary", "arbitrary"), vmem_limit_bytes=VMEM_LIMIT_BYTES),
        name="dft_stage1",
    )(wc, ws, u.reshape(b, r, GRID_W, w), ct, st)

    cos64, sin64 = _dft_tables(GRID_W)
    d = jnp.concatenate([jnp.concatenate([cos64, sin64], axis=1),
                         jnp.concatenate([-sin64, cos64], axis=1)], axis=0).astype(BF16)
    groups = min(32, r)
    f = pl.pallas_call(
        functools.partial(_dft_stage2_kernel, norm=1.0 / math.sqrt(n * F_GROUP_DIM)),
        grid=(b, r // groups),
        in_specs=[pl.BlockSpec((1, groups * GRID_W, w), lambda bi, g: (bi, g, 0)),
                  pl.BlockSpec((1, groups * GRID_W, w), lambda bi, g: (bi, g, 0)),
                  pl.BlockSpec(d.shape, lambda bi, g: (0, 0)),
                  pl.BlockSpec(bc.shape, lambda bi, g: (0, 0)),
                  pl.BlockSpec(bs.shape, lambda bi, g: (0, 0))],
        out_specs=pl.BlockSpec((1, GRID_W, groups, w), lambda bi, g: (bi, 0, g, 0)),
        out_shape=jax.ShapeDtypeStruct((b, GRID_W, r, w), F32),
        compiler_params=pltpu.CompilerParams(
            dimension_semantics=("arbitrary", "arbitrary"), vmem_limit_bytes=VMEM_LIMIT_BYTES),
        name="dft_stage2",
    )(zr.reshape(b, n, w), zi.reshape(b, n, w), d, bc, bs)
    return f.reshape(b, n, w)


def _dft_dense_kernel(cn_ref, sn_ref, u_ref, bc_ref, bs_ref, o_ref, *, norm):
    u = u_ref[0]
    pc = _dot(cn_ref[...], u).astype(BF16)
    ps = _dot(sn_ref[...], u).astype(BF16)
    o_ref[0] = (_dot(pc, bc_ref[...]) - _dot(ps, bs_ref[...])) * norm


def _pos_dft_dense(u, bc, bs):
    b, n, w = u.shape
    cos_n, sin_n = _dft_tables(n)
    const2 = lambda bi: (0, 0)
    return pl.pallas_call(
        functools.partial(_dft_dense_kernel, norm=1.0 / math.sqrt(n * F_GROUP_DIM)),
        grid=(b,),
        in_specs=[pl.BlockSpec((n, n), const2), pl.BlockSpec((n, n), const2),
                  pl.BlockSpec((1, n, w), lambda bi: (bi, 0, 0)),
                  pl.BlockSpec(bc.shape, const2), pl.BlockSpec(bs.shape, const2)],
        out_specs=pl.BlockSpec((1, n, w), lambda bi: (bi, 0, 0)),
        out_shape=jax.ShapeDtypeStruct((b, n, w), F32),
        compiler_params=pltpu.CompilerParams(
            dimension_semantics=("arbitrary",), vmem_limit_bytes=VMEM_LIMIT_BYTES),
        name="dft_dense",
    )(cos_n.astype(BF16), sin_n.astype(BF16), u, bc, bs)


def _pos_dft(u, bc, bs):
    r = u.shape[1] // GRID_W
    if r % 16 == 0:
        return _pos_dft_factored(u, bc, bs)
    return _pos_dft_dense(u, bc, bs)


def _key_chunks(k_refs):
    chunks = []
    for seg, k_ref in enumerate(k_refs):
        for c in range(k_ref.shape[2] // KEY_CHUNK):
            chunks.append((seg, c, len(chunks)))
    return chunks


def _rows(c):
    return slice(c * KEY_CHUNK, (c + 1) * KEY_CHUNK)


def _sublane_partial_sum(p_t):
    return jnp.sum(p_t.reshape(p_t.shape[0] // 8, 8, p_t.shape[1]), axis=0)


def _attention_kernel(*refs, nseg):
    qt_ref, k_refs, vt_refs = refs[0], refs[1:1 + nseg], refs[1 + nseg:1 + 2 * nseg]
    o_ref, s0_ref, s1_ref, ot_ref = refs[1 + 2 * nseg:]
    heads, tq = qt_ref.shape[1], qt_ref.shape[3]
    chunks = _key_chunks(k_refs)
    s_bufs = (s0_ref, s1_ref)

    def phase(h_prod, h_cons, m_cons, prod_buf):
        m_part, acc, l_part = None, None, None
        for seg, c, row in chunks:
            if h_prod is not None:
                s_t = _dot(k_refs[seg][0, h_prod, _rows(c), :], qt_ref[0, h_prod])
                s_bufs[prod_buf][_rows(row), :] = s_t
                pm = jnp.max(s_t.reshape(KEY_CHUNK // 8, 8, tq), axis=0)
                m_part = pm if m_part is None else jnp.maximum(m_part, pm)
            if h_cons is not None:
                p_t = jnp.exp2(s_bufs[1 - prod_buf][_rows(row), :] - m_cons)
                d = _dot(vt_refs[seg][0, h_cons, :, _rows(c)], p_t.astype(BF16))
                acc = d if acc is None else acc + d
                l_part = _sublane_partial_sum(p_t) if l_part is None else l_part + _sublane_partial_sum(p_t)
        if h_cons is not None:
            ot_ref[pl.ds(pl.multiple_of(h_cons * A_V, A_V), A_V), :] = (
                acc / jnp.sum(l_part, axis=0, keepdims=True))
        return None if h_prod is None else jnp.max(m_part, axis=0, keepdims=True)

    def head_pair(j, m):
        m = phase(2 * j + 1, 2 * j, m, 1)
        return phase(2 * j + 2, 2 * j + 1, m, 0)

    m = phase(0, None, None, 0)
    m = lax.fori_loop(0, (heads - 2) // 2, head_pair, m)
    m = phase(heads - 1, heads - 2, m, 1)
    phase(None, heads - 1, m, 0)
    o_ref[0] = ot_ref[...].T.astype(BF16)


def _attention_bounded_kernel(*refs, nseg):
    shift_ref, qt_ref, k_refs, vt_refs = refs[0], refs[1], refs[2:2 + nseg], refs[2 + nseg:2 + 2 * nseg]
    o_ref, ot_ref = refs[2 + 2 * nseg:]
    heads = qt_ref.shape[1]
    chunks = _key_chunks(k_refs)
    shift = shift_ref[0, 0]

    def head_group(j, carry):
        items = [(j * HEAD_GROUP + g, chunk) for g in range(HEAD_GROUP) for chunk in chunks]
        scores, acc, l_part = [], None, None
        for i in range(len(items) + LOOKAHEAD):
            if i < len(items):
                h, (seg, c, _) = items[i]
                scores.append(_dot(k_refs[seg][0, h, _rows(c), :], qt_ref[0, h]))
            if i >= LOOKAHEAD:
                h, (seg, c, row) = items[i - LOOKAHEAD]
                p_t = jnp.exp2(scores.pop(0) - shift)
                d = _dot(vt_refs[seg][0, h, :, _rows(c)], p_t.astype(BF16))
                acc = d if row == 0 else acc + d
                l_part = _sublane_partial_sum(p_t) if row == 0 else l_part + _sublane_partial_sum(p_t)
                if row == len(chunks) - 1:
                    ot_ref[pl.ds(pl.multiple_of(h * A_V, A_V), A_V), :] = (
                        acc / jnp.sum(l_part, axis=0, keepdims=True))
        return carry

    lax.fori_loop(0, heads // HEAD_GROUP, head_group, 0)
    o_ref[0] = ot_ref[...].T.astype(BF16)


def _attention(qt, ks, vts, tq, score_bound):
    b, heads, _, n = qt.shape
    nseg = len(ks)
    n_keys = sum(k.shape[2] for k in ks)
    kvmap = lambda bi, qi: (bi, 0, 0, 0)
    q_spec = pl.BlockSpec((1, heads, LANES, tq), lambda bi, qi: (bi, 0, 0, qi))
    o_spec = pl.BlockSpec((1, tq, heads * A_V), lambda bi, qi: (bi, qi, 0))
    o_shape = jax.ShapeDtypeStruct((b, n, heads * A_V), BF16)
    params = pltpu.CompilerParams(dimension_semantics=("arbitrary", "arbitrary"),
                                  vmem_limit_bytes=VMEM_LIMIT_BYTES)

    def kv_specs(**kw):
        return ([pl.BlockSpec((1,) + k.shape[1:], kvmap, **kw) for k in ks]
                + [pl.BlockSpec((1,) + vt.shape[1:], kvmap, **kw) for vt in vts])

    def bounded(qt, *kv):
        return pl.pallas_call(
            functools.partial(_attention_bounded_kernel, nseg=nseg),
            grid=(b, n // tq),
            in_specs=[pl.BlockSpec(memory_space=pltpu.SMEM), q_spec] + kv_specs(),
            out_specs=o_spec,
            out_shape=o_shape,
            scratch_shapes=[pltpu.VMEM((heads * A_V, tq), F32)],
            compiler_params=params,
            name="attention_bounded",
        )(jnp.reshape(score_bound, (1, 1)), qt, *kv)

    def row_max(qt, *kv):
        return pl.pallas_call(
            functools.partial(_attention_kernel, nseg=nseg),
            grid=(b, n // tq),
            in_specs=[q_spec] + kv_specs(pipeline_mode=pl.Buffered(1)),
            out_specs=o_spec,
            out_shape=o_shape,
            scratch_shapes=[pltpu.VMEM((n_keys, tq), F32), pltpu.VMEM((n_keys, tq), F32),
                            pltpu.VMEM((heads * A_V, tq), F32)],
            compiler_params=params,
            name="attention",
        )(qt, *kv)

    return lax.cond(score_bound <= MAX_SAFE_SHIFT, bounded, row_max, qt, *ks, *vts)


def _out_proj_tile(x_ref, gate_ref, f_ref, sfg_ref, gout_ref, o_ref, sag_ref, wf_ref, wo_ref):
    f_out = (_dot(f_ref[0].astype(BF16), wf_ref[...]) * sfg_ref[0].astype(F32)).astype(BF16)
    o_g = (o_ref[0].astype(F32) * sag_ref[0].astype(F32)).astype(BF16)
    y = (_dot(f_out, wo_ref[0:F_WIDTH, :])
         + _dot(gout_ref[0], wo_ref[F_WIDTH:F_WIDTH + G_WIDTH, :])
         + _dot(o_g, wo_ref[F_WIDTH + G_WIDTH:, :]))
    return x_ref[0] + gate_ref[0] * y


def _out_proj_kernel(*refs):
    refs[-1][0] = _out_proj_tile(*refs[:-1])


def _out_proj_operands(x, gate, f, sfg, gout, o, sag, w_fmix, w_out, tm):
    d = x.shape[2]
    tok = lambda bi, i: (bi, i, 0)
    mod_map = (lambda bi, i: (bi, 0, 0)) if gate.shape[0] > 1 else (lambda bi, i: (0, 0, 0))
    const2 = lambda bi, i: (0, 0)
    in_specs = [pl.BlockSpec((1, tm, d), tok),
                pl.BlockSpec((1, 1, d), mod_map),
                pl.BlockSpec((1, tm, F_WIDTH), tok),
                pl.BlockSpec((1, tm, F_WIDTH), tok),
                pl.BlockSpec((1, tm, G_WIDTH), tok),
                pl.BlockSpec((1, tm, A_WIDTH), tok),
                pl.BlockSpec((1, tm, A_WIDTH), tok),
                pl.BlockSpec(w_fmix.shape, const2),
                pl.BlockSpec(w_out.shape, const2)]
    return [x, gate, f, sfg, gout, o, sag, w_fmix, w_out], in_specs


def _out_proj(x, gate, f, sfg, gout, o, sag, w_fmix, w_out, tm):
    b, n, d = x.shape
    args, in_specs = _out_proj_operands(x, gate, f, sfg, gout, o, sag, w_fmix, w_out, tm)
    return pl.pallas_call(
        _out_proj_kernel,
        grid=(b, n // tm),
        in_specs=in_specs,
        out_specs=pl.BlockSpec((1, tm, d), lambda bi, i: (bi, i, 0)),
        out_shape=jax.ShapeDtypeStruct((b, n, d), F32),
        compiler_params=pltpu.CompilerParams(
            dimension_semantics=("arbitrary", "arbitrary"), vmem_limit_bytes=VMEM_LIMIT_BYTES),
        name="out_proj",
    )(*args)


def _rot_perm_sign():
    q = A_ROPE // 4
    src, sign = [], []
    for j in range(A_ROPE):
        blk, r = divmod(j, q)
        if blk % 2 == 0:
            src.append((blk + 1) * q + r)
            sign.append(-1.0)
        else:
            src.append((blk - 1) * q + r)
            sign.append(1.0)
    return jnp.array(src, jnp.int32), jnp.array(sign, F32)


def _dft_tables(n):
    idx = (jnp.arange(n, dtype=jnp.int32)[:, None] * jnp.arange(n, dtype=jnp.int32)[None, :]) % n
    ang = idx.astype(F32) * (2.0 * math.pi / n)
    return jnp.cos(ang), jnp.sin(ang)


def _rope_cos_sin(n):
    rows = n // GRID_W
    row = jnp.repeat(jnp.arange(rows, dtype=F32), GRID_W)
    col = jnp.tile(jnp.arange(GRID_W, dtype=F32), rows)
    half = A_ROPE // 2
    inv = ROPE_BASE ** (-jnp.arange(0, half, 2, dtype=F32) / half)
    ang_r = row[:, None] * inv[None, :]
    ang_c = col[:, None] * inv[None, :]
    ang = jnp.concatenate([ang_r, ang_r, ang_c, ang_c], axis=-1)
    return jnp.cos(ang), jnp.sin(ang)


def _head_tables(gain, cos, sin, src):
    n = cos.shape[0]
    g_nope = jnp.broadcast_to(gain[:A_NOPE][None, :], (n, A_NOPE))
    a = gain[A_NOPE:][None, :] * cos
    b = gain[A_NOPE:][src][None, :] * sin
    tab1 = jnp.concatenate([g_nope, a, b], axis=-1)
    tab2 = jnp.concatenate([jnp.zeros((n, A_NOPE), F32), b, a], axis=-1)
    return tab1, tab2


def _layer_weights(l, p, src, sign):
    w_in = p["w_in"][l]
    d = w_in.shape[0]
    sizes = (F_WIDTH, F_WIDTH, G_WIDTH, G_WIDTH, G_WIDTH, Q_LORA, KV_LORA, A_ROPE, A_WIDTH)
    parts, start = [], 0
    for s in sizes:
        parts.append(w_in[:, start:start + s])
        start += s
    w_kr = parts[7]
    w_rot = w_kr[:, src] * sign[None, :]
    kr_tile = jnp.concatenate([jnp.zeros((d, A_NOPE), F32), w_kr, w_rot], axis=-1)
    kr2_tile = jnp.concatenate([jnp.zeros((d, A_NOPE), F32), w_rot, w_kr], axis=-1)
    w_in_p = jnp.concatenate(parts[:7] + [kr_tile, kr2_tile, parts[8]], axis=-1).astype(BF16)

    w_uq = p["w_uq"][l].reshape(Q_LORA, A_HEADS, A_QK)
    w_uq_rope = w_uq[:, :, A_NOPE:]
    w_uq_p = jnp.concatenate([w_uq, w_uq_rope[:, :, src] * sign[None, None, :]], axis=-1)
    w_uq_t = w_uq_p.reshape(Q_LORA, A_HEADS * LANES).T.astype(BF16)

    w_ukv = p["w_ukv"][l].reshape(KV_LORA, A_HEADS, A_NOPE + A_V)
    w_uk_p = jnp.concatenate([w_ukv[:, :, :A_NOPE], jnp.zeros((KV_LORA, A_HEADS, LANES - A_NOPE), F32)], axis=-1)
    w_uk_p = w_uk_p.reshape(KV_LORA, A_HEADS * LANES).astype(BF16)
    w_uv_t = w_ukv[:, :, A_NOPE:].reshape(KV_LORA, A_WIDTH).T.astype(BF16)

    ws = p["g_ws"][l]
    ws_cat = jnp.transpose(ws, (1, 0, 2)).reshape(CHUNK, G_HEADS * CHUNK).astype(BF16)
    bs_tab = jnp.repeat(p["g_bs"][l].T, G_HEAD_DIM, axis=1)
    return {
        "norm_g": p["norm_g"][l][None, :], "w_in": w_in_p, "w_uq_t": w_uq_t, "w_uk": w_uk_p, "w_uv_t": w_uv_t,
        "g_ln_g": jnp.tile(p["g_ln_g"][l], G_HEADS)[None, :], "ws_cat": ws_cat, "bs_tab": bs_tab,
        "q_a_g": p["q_a_g"][l][None, :], "kv_a_g": p["kv_a_g"][l][None, :],
        "w_fmix": p["w_fmix"][l].astype(BF16), "w_out": p["w_out"][l].astype(BF16),
    }


def _token_tile(n):
    return 512 if n % 512 == 0 else 256


def kernel(x, c, ctx, c_ctx, w_mod, b_mod, norm_g, w_in, w_fmix, g_ln_g, g_ws, g_bs,
           q_a_g, w_uq, kv_a_g, w_ukv, q_norm_g, k_norm_g, w_out):
    p = dict(w_in=w_in, w_fmix=w_fmix, g_ln_g=g_ln_g, g_ws=g_ws, g_bs=g_bs, q_a_g=q_a_g, w_uq=w_uq,
             kv_a_g=kv_a_g, w_ukv=w_ukv, norm_g=norm_g, w_out=w_out)
    depth = w_mod.shape[0]
    b, n, d = x.shape
    n_ctx = ctx.shape[1]
    src, sign = _rot_perm_sign()

    rows = -(-(b + 1) // 8) * 8
    cc = jnp.concatenate([c, c_ctx[None, :], jnp.zeros((rows - b - 1, d), F32)], axis=0)
    mod = _modulation(_silu(cc), w_mod, b_mod)

    cos64, sin64 = _dft_tables(F_GROUP_DIM)
    eye_g = jnp.eye(F_GROUPS, dtype=F32)
    bc = jnp.kron(eye_g, cos64).astype(BF16)
    bs = jnp.kron(eye_g, sin64).astype(BF16)
    jm = jnp.kron(jnp.eye(G_HEADS, dtype=F32), jnp.full((G_HEAD_DIM, G_HEAD_DIM), 1.0 / G_HEAD_DIM, F32)).astype(BF16)
    cos_x, sin_x = _rope_cos_sin(n)
    cos_y, sin_y = jnp.ones((n_ctx, A_ROPE), F32), jnp.zeros((n_ctx, A_ROPE), F32)
    q_scale = (A_QK ** -0.5) * math.log2(math.e)

    y, prev_x, prev_y = ctx, None, None
    for l in range(depth):
        lw = _layer_weights(l, p, src, sign)
        lw.update(jm=jm)
        shift, scale, gate = (mod[l, :, i * d:(i + 1) * d] for i in range(3))
        tabs_x, tabs_y = {}, {}
        for tabs, cs, sn in ((tabs_x, cos_x, sin_x), (tabs_y, cos_y, sin_y)):
            tabs["qt"] = (_head_tables(q_norm_g[l], cs, sn, src)[0] * q_scale).T
            tabs["k1"], tabs["k2"] = _head_tables(k_norm_g[l], cs, sn, src)

        score_bound = (1.02 * A_QK * q_scale) * jnp.max(jnp.abs(q_norm_g[l])) * jnp.max(jnp.abs(k_norm_g[l]))

        tm_y, tm_x = _token_tile(n_ctx), _token_tile(n)
        outs_y = _in_proj(y, scale[b:b + 1, None, :], shift[b:b + 1, None, :], lw, tabs_y, tm_y, prev_y)
        outs_x = _in_proj(x, scale[:b, None, :], shift[:b, None, :], lw, tabs_x, tm_x, prev_x)
        if l > 0:
            y, x = outs_y[0], outs_x[0]
            outs_y, outs_x = outs_y[1:], outs_x[1:]
        fin_y, sfg_y, gout_y, q_y, k_y, vt_y, sag_y = outs_y
        fin_x, sfg_x, gout_x, q_x, k_x, vt_x, sag_x = outs_x

        f_x = _pos_dft(fin_x, bc, bs)
        o_x = _attention(q_x, [k_x, k_y], [vt_x, vt_y], tm_x, score_bound)
        prev_x = (gate[:b, None, :], f_x, sfg_x, gout_x, o_x, sag_x, lw["w_fmix"], lw["w_out"])
        if l < depth - 1:
            f_y = _pos_dft(fin_y, bc, bs)
            o_y = _attention(q_y, [k_y], [vt_y], tm_y, score_bound)
            prev_y = (gate[b:b + 1, None, :], f_y, sfg_y, gout_y, o_y, sag_y, lw["w_fmix"], lw["w_out"])
    return _out_proj(x, *prev_x, _token_tile(n))
```

```python
import functools
import math

import jax
import jax.numpy as jnp
from jax import lax
from jax.experimental import pallas as pl
from jax.experimental.pallas import tpu as pltpu

F32 = jnp.float32
BF16 = jnp.bfloat16

LANES = 128
VMEM_LIMIT_BYTES = 56 * 1024 * 1024

GRID_W = 64
F_GROUPS = 4
F_GROUP_DIM = 64
F_WIDTH = F_GROUPS * F_GROUP_DIM
G_HEADS = 4
G_HEAD_DIM = 64
G_WIDTH = G_HEADS * G_HEAD_DIM
CHUNK = 128
A_HEADS = 8
A_NOPE = 64
A_ROPE = 32
A_V = 64
A_QK = A_NOPE + A_ROPE
A_WIDTH = A_HEADS * A_V
Q_LORA = 256
KV_LORA = 128
ROPE_BASE = 10000.0
EPS = 1e-6
KEY_CHUNK = 256
MAX_SAFE_SHIFT = 60.0
HEAD_GROUP = 8
LOOKAHEAD = 2

OFF_FIN = 0
OFF_FGATE = OFF_FIN + F_WIDTH
OFF_GU = OFF_FGATE + F_WIDTH
OFF_GV = OFF_GU + G_WIDTH
OFF_GGATE = OFF_GV + G_WIDTH
OFF_CQ = OFF_GGATE + G_WIDTH
OFF_CKV = OFF_CQ + Q_LORA
OFF_KR = OFF_CKV + KV_LORA
OFF_KR2 = OFF_KR + LANES
OFF_AGATE = OFF_KR2 + LANES
IN_P = OFF_AGATE + A_WIDTH


def _silu(t):
    return t * (1.0 / (1.0 + jnp.exp(-t)))


def _dot(a, b):
    return jnp.dot(a, b, preferred_element_type=F32)


def _dot_hilo(t, w):
    hi = t.astype(BF16)
    lo = (t - hi.astype(F32)).astype(BF16)
    return _dot(hi, w) + _dot(lo, w)


def _mod_kernel(cc_ref, w_ref, b_ref, o_ref):
    o_ref[0] = lax.dot_general(cc_ref[...], w_ref[0], (((1,), (0,)), ((), ())),
                               precision=lax.Precision.HIGHEST,
                               preferred_element_type=F32) + b_ref[0]


def _modulation(cc, w_mod, b_mod):
    depth, d, d3 = w_mod.shape
    rows = cc.shape[0]
    tn = 1024
    return pl.pallas_call(
        _mod_kernel,
        grid=(depth, d3 // tn),
        in_specs=[pl.BlockSpec((rows, d), lambda l, j: (0, 0)),
                  pl.BlockSpec((1, d, tn), lambda l, j: (l, 0, j)),
                  pl.BlockSpec((1, 1, tn), lambda l, j: (l, 0, j))],
        out_specs=pl.BlockSpec((1, rows, tn), lambda l, j: (l, 0, j)),
        out_shape=jax.ShapeDtypeStruct((depth, rows, d3), F32),
        compiler_params=pltpu.CompilerParams(vmem_limit_bytes=VMEM_LIMIT_BYTES),
        name="modulation",
    )(cc, w_mod, b_mod.reshape(depth, 1, d3))


def _in_proj_kernel(*refs, fused):
    if fused:
        x = _out_proj_tile(*refs[:9])
        refs = refs[9:]
    else:
        x = refs[0][0]
        refs = refs[1:]
    (sc_ref, sh_ref, ng_ref, win_ref, jm_ref, gln_ref, ws_ref, bsb_ref, qag_ref, wuqt_ref, kvag_ref,
     wuk_ref, wuvt_ref, tqt_ref, tk1_ref, tk2_ref) = refs[:16]
    outs = refs[16:]
    if fused:
        outs[0][0] = x
        outs = outs[1:]
    fin_ref, sfg_ref, gout_ref, q_ref, k_ref, vt_ref, sag_ref = outs
    tm = x.shape[0]
    xn = x * lax.rsqrt(jnp.mean(x * x, axis=-1, keepdims=True) + EPS) * ng_ref[...]
    hb = (xn * (1.0 + sc_ref[0]) + sh_ref[0]).astype(BF16)

    def proj(off, width):
        return _dot(hb, win_ref[:, off:off + width])

    g_v = proj(OFF_GV, G_WIDTH)
    c_q = proj(OFF_CQ, Q_LORA)
    ckv_kr = proj(OFF_CKV, KV_LORA + LANES)
    c_kv, kr = ckv_kr[:, 0:KV_LORA], ckv_kr[:, KV_LORA:]
    kr2 = proj(OFF_KR2, LANES) * tk2_ref[...]
    fin_ref[0] = proj(OFF_FIN, F_WIDTH).astype(BF16)
    jm = jm_ref[...]
    dv = g_v - _dot_hilo(g_v, jm)
    sfg_ref[0] = _silu(proj(OFF_FGATE, F_WIDTH)).astype(BF16)
    var = _dot_hilo(dv * dv, jm)
    g_u = proj(OFF_GU, G_WIDTH)
    g_gate = proj(OFF_GGATE, G_WIDTH)

    cqn = c_q * lax.rsqrt(jnp.mean(c_q * c_q, axis=-1, keepdims=True) + EPS) * qag_ref[...]
    qt_all = _dot(wuqt_ref[...], cqn.T.astype(BF16))
    tqt = tqt_ref[...]
    for h in range(A_HEADS):
        t = qt_all[h * LANES:(h + 1) * LANES]
        ss = jnp.sum(t[0:A_QK] * t[0:A_QK], axis=0, keepdims=True)
        q_ref[0, h] = (t * lax.rsqrt(ss * (1.0 / A_QK) + EPS) * tqt).astype(BF16)

    lane = lax.broadcasted_iota(jnp.int32, (1, LANES), 1)
    lane_mask = (lane < A_QK).astype(F32)
    ckvn = c_kv * lax.rsqrt(jnp.mean(c_kv * c_kv, axis=-1, keepdims=True) + EPS) * kvag_ref[...]
    ckvn_b = ckvn.astype(BF16)
    tk1 = tk1_ref[...]
    for h in range(0, A_HEADS, 2):
        t_pair = _dot(ckvn_b, wuk_ref[:, h * LANES:(h + 2) * LANES])
        for g in range(2):
            t = t_pair[:, g * LANES:(g + 1) * LANES] + kr
            ss = jnp.sum(t * t * lane_mask, axis=-1, keepdims=True)
            k_ref[0, h + g] = ((t * tk1 + kr2) * lax.rsqrt(ss * (1.0 / A_QK) + EPS)).astype(BF16)
    vt = _dot(wuvt_ref[...], ckvn.T.astype(BF16))
    vt_ref[0] = vt.astype(BF16).reshape(A_HEADS, A_V, tm)

    sag_ref[0] = _silu(proj(OFF_AGATE, A_WIDTH)).astype(BF16)

    vln = dv * lax.rsqrt(var + EPS) * gln_ref[...]
    head_of_lane = lax.broadcasted_iota(jnp.int32, (CHUNK, G_WIDTH), 1) // G_HEAD_DIM
    for c in range(tm // CHUNK):
        rows = slice(c * CHUNK, (c + 1) * CHUNK)
        vc = vln[rows]
        stacked = jnp.concatenate(
            [jnp.where(head_of_lane == h, vc, 0.0).astype(BF16) for h in range(G_HEADS)], axis=0)
        mixed = _dot(ws_ref[...], stacked) + bsb_ref[...]
        gout_ref[0, rows, :] = (g_u[rows] * mixed * _silu(g_gate[rows])).astype(BF16)


def _in_proj(x, scale, shift, lw, tabs, tm, prev=None):
    b, n, d = x.shape
    mod_map = (lambda bi, i: (bi, 0, 0)) if scale.shape[0] > 1 else (lambda bi, i: (0, 0, 0))
    const2 = lambda bi, i: (0, 0)
    row_tab = lambda bi, i: (i, 0)
    tok = lambda bi, i: (bi, i, 0)
    full = lambda a: pl.BlockSpec(a.shape, const2)
    if prev is None:
        args, in_specs = [x], [pl.BlockSpec((1, tm, d), tok)]
    else:
        args, in_specs = _out_proj_operands(x, *prev, tm)
    weights = [lw["norm_g"], lw["w_in"], lw["jm"], lw["g_ln_g"], lw["ws_cat"], lw["bs_tab"], lw["q_a_g"],
               lw["w_uq_t"], lw["kv_a_g"], lw["w_uk"], lw["w_uv_t"]]
    args += [scale, shift] + weights + [tabs["qt"], tabs["k1"], tabs["k2"]]
    in_specs += [pl.BlockSpec((1, 1, d), mod_map), pl.BlockSpec((1, 1, d), mod_map)]
    in_specs += [full(a) for a in weights]
    in_specs += [pl.BlockSpec((LANES, tm), lambda bi, i: (0, i)),
                 pl.BlockSpec((tm, LANES), row_tab), pl.BlockSpec((tm, LANES), row_tab)]
    out_shape = [
        jax.ShapeDtypeStruct((b, n, F_WIDTH), BF16),
        jax.ShapeDtypeStruct((b, n, F_WIDTH), BF16),
        jax.ShapeDtypeStruct((b, n, G_WIDTH), BF16),
        jax.ShapeDtypeStruct((b, A_HEADS, LANES, n), BF16),
        jax.ShapeDtypeStruct((b, A_HEADS, n, LANES), BF16),
        jax.ShapeDtypeStruct((b, A_HEADS, A_V, n), BF16),
        jax.ShapeDtypeStruct((b, n, A_WIDTH), BF16),
    ]
    out_specs = [
        pl.BlockSpec((1, tm, F_WIDTH), tok),
        pl.BlockSpec((1, tm, F_WIDTH), tok),
        pl.BlockSpec((1, tm, G_WIDTH), tok),
        pl.BlockSpec((1, A_HEADS, LANES, tm), lambda bi, i: (bi, 0, 0, i)),
        pl.BlockSpec((1, A_HEADS, tm, LANES), lambda bi, i: (bi, 0, i, 0)),
        pl.BlockSpec((1, A_HEADS, A_V, tm), lambda bi, i: (bi, 0, 0, i)),
        pl.BlockSpec((1, tm, A_WIDTH), tok),
    ]
    if prev is not None:
        out_shape = [jax.ShapeDtypeStruct((b, n, d), F32)] + out_shape
        out_specs = [pl.BlockSpec((1, tm, d), tok)] + out_specs
    return pl.pallas_call(
        functools.partial(_in_proj_kernel, fused=prev is not None),
        grid=(b, n // tm),
        in_specs=in_specs,
        out_specs=out_specs,
        out_shape=out_shape,
        compiler_params=pltpu.CompilerParams(
            dimension_semantics=("arbitrary", "arbitrary"), vmem_limit_bytes=VMEM_LIMIT_BYTES),
        name="in_proj_fused" if prev is not None else "in_proj",
    )(*args)


DFT_T = 16


def _dft_stage1_kernel(wc_ref, ws_ref, u_ref, ct_ref, st_ref, zr_ref, zi_ref):
    r, t, w = u_ref.shape[1:]
    u = u_ref[0].reshape(r * t, w)
    yr = _dot(wc_ref[...], u)
    yi = _dot(ws_ref[...], u)
    ct, st = ct_ref[...].reshape(r * t, w), st_ref[...].reshape(r * t, w)
    zr_ref[0] = (yr * ct + yi * st).astype(BF16).reshape(r, t, w)
    zi_ref[0] = (yi * ct - yr * st).astype(BF16).reshape(r, t, w)


def _dft_stage2_kernel(zr_ref, zi_ref, d_ref, bc_ref, bs_ref, o_ref, *, norm):
    groups = o_ref.shape[2]
    xs = []
    for j in range(groups):
        rows = slice(j * GRID_W, (j + 1) * GRID_W)
        z = jnp.concatenate([zr_ref[0, rows, :], zi_ref[0, rows, :]], axis=0)
        xs.append(_dot(d_ref[...], z))
    xr = jnp.concatenate([x[0:GRID_W] for x in xs], axis=0).astype(BF16)
    xi = jnp.concatenate([x[GRID_W:] for x in xs], axis=0).astype(BF16)
    f = (_dot(xr, bc_ref[...]) + _dot(xi, bs_ref[...])) * norm
    for j in range(groups):
        o_ref[0, :, j, :] = f[j * GRID_W:(j + 1) * GRID_W]


def _pos_dft_factored(u, bc, bs):
    b, n, w = u.shape
    r = n // GRID_W
    i = jnp.arange(r * DFT_T, dtype=jnp.int32)
    k1, t = i // DFT_T, i % DFT_T
    ang_r = ((k1[:, None] * k1[None, :]) % r).astype(F32) * (2.0 * math.pi / r)
    same_t = t[:, None] == t[None, :]
    wc = jnp.where(same_t, jnp.cos(ang_r), 0.0).astype(BF16)
    ws = jnp.where(same_t, -jnp.sin(ang_r), 0.0).astype(BF16)
    ang = (jnp.arange(r, dtype=jnp.int32)[:, None] * jnp.arange(GRID_W, dtype=jnp.int32)[None, :]
           ).astype(F32) * (2.0 * math.pi / n)
    ct = jnp.broadcast_to(jnp.cos(ang)[:, :, None], (r, GRID_W, w))
    st = jnp.broadcast_to(jnp.sin(ang)[:, :, None], (r, GRID_W, w))
    blk = lambda j, bi: (bi, 0, j, 0)
    zr, zi = pl.pallas_call(
        _dft_stage1_kernel,
        grid=(GRID_W // DFT_T, b),
        in_specs=[pl.BlockSpec(wc.shape, lambda j, bi: (0, 0)),
                  pl.BlockSpec(ws.shape, lambda j, bi: (0, 0)),
                  pl.BlockSpec((1, r, DFT_T, w), blk),
                  pl.BlockSpec((r, DFT_T, w), lambda j, bi: (0, j, 0)),
                  pl.BlockSpec((r, DFT_T, w), lambda j, bi: (0, j, 0))],
        out_specs=[pl.BlockSpec((1, r, DFT_T, w), blk), pl.BlockSpec((1, r, DFT_T, w), blk)],
        out_shape=[jax.ShapeDtypeStruct((b, r, GRID_W, w), BF16)] * 2,
        compiler_params=pltpu.CompilerParams(
            dimension_semantics=("arbitrary", "arbitrary"), vmem_limit_bytes=VMEM_LIMIT_BYTES),
        name="dft_stage1",
    )(wc, ws, u.reshape(b, r, GRID_W, w), ct, st)

    cos64, sin64 = _dft_tables(GRID_W)
    d = jnp.concatenate([jnp.concatenate([cos64, sin64], axis=1),
                         jnp.concatenate([-sin64, cos64], axis=1)], axis=0).astype(BF16)
    groups = min(32, r)
    f = pl.pallas_call(
        functools.partial(_dft_stage2_kernel, norm=1.0 / math.sqrt(n * F_GROUP_DIM)),
        grid=(b, r // groups),
        in_specs=[pl.BlockSpec((1, groups * GRID_W, w), lambda bi, g: (bi, g, 0)),
                  pl.BlockSpec((1, groups * GRID_W, w), lambda bi, g: (bi, g, 0)),
                  pl.BlockSpec(d.shape, lambda bi, g: (0, 0)),
                  pl.BlockSpec(bc.shape, lambda bi, g: (0, 0)),
                  pl.BlockSpec(bs.shape, lambda bi, g: (0, 0))],
        out_specs=pl.BlockSpec((1, GRID_W, groups, w), lambda bi, g: (bi, 0, g, 0)),
        out_shape=jax.ShapeDtypeStruct((b, GRID_W, r, w), F32),
        compiler_params=pltpu.CompilerParams(
            dimension_semantics=("arbitrary", "arbitrary"), vmem_limit_bytes=VMEM_LIMIT_BYTES),
        name="dft_stage2",
    )(zr.reshape(b, n, w), zi.reshape(b, n, w), d, bc, bs)
    return f.reshape(b, n, w)


def _dft_dense_kernel(cn_ref, sn_ref, u_ref, bc_ref, bs_ref, o_ref, *, norm):
    u = u_ref[0]
    pc = _dot(cn_ref[...], u).astype(BF16)
    ps = _dot(sn_ref[...], u).astype(BF16)
    o_ref[0] = (_dot(pc, bc_ref[...]) - _dot(ps, bs_ref[...])) * norm


def _pos_dft_dense(u, bc, bs):
    b, n, w = u.shape
    cos_n, sin_n = _dft_tables(n)
    const2 = lambda bi: (0, 0)
    return pl.pallas_call(
        functools.partial(_dft_dense_kernel, norm=1.0 / math.sqrt(n * F_GROUP_DIM)),
        grid=(b,),
        in_specs=[pl.BlockSpec((n, n), const2), pl.BlockSpec((n, n), const2),
                  pl.BlockSpec((1, n, w), lambda bi: (bi, 0, 0)),
                  pl.BlockSpec(bc.shape, const2), pl.BlockSpec(bs.shape, const2)],
        out_specs=pl.BlockSpec((1, n, w), lambda bi: (bi, 0, 0)),
        out_shape=jax.ShapeDtypeStruct((b, n, w), F32),
        compiler_params=pltpu.CompilerParams(
            dimension_semantics=("arbitrary",), vmem_limit_bytes=VMEM_LIMIT_BYTES),
        name="dft_dense",
    )(cos_n.astype(BF16), sin_n.astype(BF16), u, bc, bs)


def _pos_dft(u, bc, bs):
    r = u.shape[1] // GRID_W
    if r % 16 == 0:
        return _pos_dft_factored(u, bc, bs)
    return _pos_dft_dense(u, bc, bs)


def _key_chunks(k_refs):
    chunks = []
    for seg, k_ref in enumerate(k_refs):
        for c in range(k_ref.shape[2] // KEY_CHUNK):
            chunks.append((seg, c, len(chunks)))
    return chunks


def _rows(c):
    return slice(c * KEY_CHUNK, (c + 1) * KEY_CHUNK)


def _sublane_partial_sum(p_t):
    return jnp.sum(p_t.reshape(p_t.shape[0] // 8, 8, p_t.shape[1]), axis=0)


def _attention_kernel(*refs, nseg):
    qt_ref, k_refs, vt_refs = refs[0], refs[1:1 + nseg], refs[1 + nseg:1 + 2 * nseg]
    o_ref, s0_ref, s1_ref, ot_ref = refs[1 + 2 * nseg:]
    heads, tq = qt_ref.shape[1], qt_ref.shape[3]
    chunks = _key_chunks(k_refs)
    s_bufs = (s0_ref, s1_ref)

    def phase(h_prod, h_cons, m_cons, prod_buf):
        m_part, acc, l_part = None, None, None
        for seg, c, row in chunks:
            if h_prod is not None:
                s_t = _dot(k_refs[seg][0, h_prod, _rows(c), :], qt_ref[0, h_prod])
                s_bufs[prod_buf][_rows(row), :] = s_t
                pm = jnp.max(s_t.reshape(KEY_CHUNK // 8, 8, tq), axis=0)
                m_part = pm if m_part is None else jnp.maximum(m_part, pm)
            if h_cons is not None:
                p_t = jnp.exp2(s_bufs[1 - prod_buf][_rows(row), :] - m_cons)
                d = _dot(vt_refs[seg][0, h_cons, :, _rows(c)], p_t.astype(BF16))
                acc = d if acc is None else acc + d
                l_part = _sublane_partial_sum(p_t) if l_part is None else l_part + _sublane_partial_sum(p_t)
        if h_cons is not None:
            ot_ref[pl.ds(pl.multiple_of(h_cons * A_V, A_V), A_V), :] = (
                acc / jnp.sum(l_part, axis=0, keepdims=True))
        return None if h_prod is None else jnp.max(m_part, axis=0, keepdims=True)

    def head_pair(j, m):
        m = phase(2 * j + 1, 2 * j, m, 1)
        return phase(2 * j + 2, 2 * j + 1, m, 0)

    m = phase(0, None, None, 0)
    m = lax.fori_loop(0, (heads - 2) // 2, head_pair, m)
    m = phase(heads - 1, heads - 2, m, 1)
    phase(None, heads - 1, m, 0)
    o_ref[0] = ot_ref[...].T.astype(BF16)


def _attention_bounded_kernel(*refs, nseg):
    shift_ref, qt_ref, k_refs, vt_refs = refs[0], refs[1], refs[2:2 + nseg], refs[2 + nseg:2 + 2 * nseg]
    o_ref, ot_ref = refs[2 + 2 * nseg:]
    heads = qt_ref.shape[1]
    chunks = _key_chunks(k_refs)
    shift = shift_ref[0, 0]

    def head_group(j, carry):
        items = [(j * HEAD_GROUP + g, chunk) for g in range(HEAD_GROUP) for chunk in chunks]
        scores, acc, l_part = [], None, None
        for i in range(len(items) + LOOKAHEAD):
            if i < len(items):
                h, (seg, c, _) = items[i]
                scores.append(_dot(k_refs[seg][0, h, _rows(c), :], qt_ref[0, h]))
            if i >= LOOKAHEAD:
                h, (seg, c, row) = items[i - LOOKAHEAD]
                p_t = jnp.exp2(scores.pop(0) - shift)
                d = _dot(vt_refs[seg][0, h, :, _rows(c)], p_t.astype(BF16))
                acc = d if row == 0 else acc + d
                l_part = _sublane_partial_sum(p_t) if row == 0 else l_part + _sublane_partial_sum(p_t)
                if row == len(chunks) - 1:
                    ot_ref[pl.ds(pl.multiple_of(h * A_V, A_V), A_V), :] = (
                        acc / jnp.sum(l_part, axis=0, keepdims=True))
        return carry

    lax.fori_loop(0, heads // HEAD_GROUP, head_group, 0)
    o_ref[0] = ot_ref[...].T.astype(BF16)


def _attention(qt, ks, vts, tq, score_bound):
    b, heads, _, n = qt.shape
    nseg = len(ks)
    n_keys = sum(k.shape[2] for k in ks)
    kvmap = lambda bi, qi: (bi, 0, 0, 0)
    q_spec = pl.BlockSpec((1, heads, LANES, tq), lambda bi, qi: (bi, 0, 0, qi))
    o_spec = pl.BlockSpec((1, tq, heads * A_V), lambda bi, qi: (bi, qi, 0))
    o_shape = jax.ShapeDtypeStruct((b, n, heads * A_V), BF16)
    params = pltpu.CompilerParams(dimension_semantics=("arbitrary", "arbitrary"),
                                  vmem_limit_bytes=VMEM_LIMIT_BYTES)

    def kv_specs(**kw):
        return ([pl.BlockSpec((1,) + k.shape[1:], kvmap, **kw) for k in ks]
                + [pl.BlockSpec((1,) + vt.shape[1:], kvmap, **kw) for vt in vts])

    def bounded(qt, *kv):
        return pl.pallas_call(
            functools.partial(_attention_bounded_kernel, nseg=nseg),
            grid=(b, n // tq),
            in_specs=[pl.BlockSpec(memory_space=pltpu.SMEM), q_spec] + kv_specs(),
            out_specs=o_spec,
            out_shape=o_shape,
            scratch_shapes=[pltpu.VMEM((heads * A_V, tq), F32)],
            compiler_params=params,
            name="attention_bounded",
        )(jnp.reshape(score_bound, (1, 1)), qt, *kv)

    def row_max(qt, *kv):
        return pl.pallas_call(
            functools.partial(_attention_kernel, nseg=nseg),
            grid=(b, n // tq),
            in_specs=[q_spec] + kv_specs(pipeline_mode=pl.Buffered(1)),
            out_specs=o_spec,
            out_shape=o_shape,
            scratch_shapes=[pltpu.VMEM((n_keys, tq), F32), pltpu.VMEM((n_keys, tq), F32),
                            pltpu.VMEM((heads * A_V, tq), F32)],
            compiler_params=params,
            name="attention",
        )(qt, *kv)

    return lax.cond(score_bound <= MAX_SAFE_SHIFT, bounded, row_max, qt, *ks, *vts)


def _out_proj_tile(x_ref, gate_ref, f_ref, sfg_ref, gout_ref, o_ref, sag_ref, wf_ref, wo_ref):
    f_out = (_dot(f_ref[0].astype(BF16), wf_ref[...]) * sfg_ref[0].astype(F32)).astype(BF16)
    o_g = (o_ref[0].astype(F32) * sag_ref[0].astype(F32)).astype(BF16)
    y = (_dot(f_out, wo_ref[0:F_WIDTH, :])
         + _dot(gout_ref[0], wo_ref[F_WIDTH:F_WIDTH + G_WIDTH, :])
         + _dot(o_g, wo_ref[F_WIDTH + G_WIDTH:, :]))
    return x_ref[0] + gate_ref[0] * y


def _out_proj_kernel(*refs):
    refs[-1][0] = _out_proj_tile(*refs[:-1])


def _out_proj_operands(x, gate, f, sfg, gout, o, sag, w_fmix, w_out, tm):
    d = x.shape[2]
    tok = lambda bi, i: (bi, i, 0)
    mod_map = (lambda bi, i: (bi, 0, 0)) if gate.shape[0] > 1 else (lambda bi, i: (0, 0, 0))
    const2 = lambda bi, i: (0, 0)
    in_specs = [pl.BlockSpec((1, tm, d), tok),
                pl.BlockSpec((1, 1, d), mod_map),
                pl.BlockSpec((1, tm, F_WIDTH), tok),
                pl.BlockSpec((1, tm, F_WIDTH), tok),
                pl.BlockSpec((1, tm, G_WIDTH), tok),
                pl.BlockSpec((1, tm, A_WIDTH), tok),
                pl.BlockSpec((1, tm, A_WIDTH), tok),
                pl.BlockSpec(w_fmix.shape, const2),
                pl.BlockSpec(w_out.shape, const2)]
    return [x, gate, f, sfg, gout, o, sag, w_fmix, w_out], in_specs


def _out_proj(x, gate, f, sfg, gout, o, sag, w_fmix, w_out, tm):
    b, n, d = x.shape
    args, in_specs = _out_proj_operands(x, gate, f, sfg, gout, o, sag, w_fmix, w_out, tm)
    return pl.pallas_call(
        _out_proj_kernel,
        grid=(b, n // tm),
        in_specs=in_specs,
        out_specs=pl.BlockSpec((1, tm, d), lambda bi, i: (bi, i, 0)),
        out_shape=jax.ShapeDtypeStruct((b, n, d), F32),
        compiler_params=pltpu.CompilerParams(
            dimension_semantics=("arbitrary", "arbitrary"), vmem_limit_bytes=VMEM_LIMIT_BYTES),
        name="out_proj",
    )(*args)


def _rot_perm_sign():
    q = A_ROPE // 4
    src, sign = [], []
    for j in range(A_ROPE):
        blk, r = divmod(j, q)
        if blk % 2 == 0:
            src.append((blk + 1) * q + r)
            sign.append(-1.0)
        else:
            src.append((blk - 1) * q + r)
            sign.append(1.0)
    return jnp.array(src, jnp.int32), jnp.array(sign, F32)


def _dft_tables(n):
    idx = (jnp.arange(n, dtype=jnp.int32)[:, None] * jnp.arange(n, dtype=jnp.int32)[None, :]) % n
    ang = idx.astype(F32) * (2.0 * math.pi / n)
    return jnp.cos(ang), jnp.sin(ang)


def _rope_cos_sin(n):
    rows = n // GRID_W
    row = jnp.repeat(jnp.arange(rows, dtype=F32), GRID_W)
    col = jnp.tile(jnp.arange(GRID_W, dtype=F32), rows)
    half = A_ROPE // 2
    inv = ROPE_BASE ** (-jnp.arange(0, half, 2, dtype=F32) / half)
    ang_r = row[:, None] * inv[None, :]
    ang_c = col[:, None] * inv[None, :]
    ang = jnp.concatenate([ang_r, ang_r, ang_c, ang_c], axis=-1)
    return jnp.cos(ang), jnp.sin(ang)


def _head_tables(gain, cos, sin, src):
    n = cos.shape[0]
    g_nope = jnp.broadcast_to(gain[:A_NOPE][None, :], (n, A_NOPE))
    a = gain[A_NOPE:][None, :] * cos
    b = gain[A_NOPE:][src][None, :] * sin
    tab1 = jnp.concatenate([g_nope, a, b], axis=-1)
    tab2 = jnp.concatenate([jnp.zeros((n, A_NOPE), F32), b, a], axis=-1)
    return tab1, tab2


def _layer_weights(l, p, src, sign):
    w_in = p["w_in"][l]
    d = w_in.shape[0]
    sizes = (F_WIDTH, F_WIDTH, G_WIDTH, G_WIDTH, G_WIDTH, Q_LORA, KV_LORA, A_ROPE, A_WIDTH)
    parts, start = [], 0
    for s in sizes:
        parts.append(w_in[:, start:start + s])
        start += s
    w_kr = parts[7]
    w_rot = w_kr[:, src] * sign[None, :]
    kr_tile = jnp.concatenate([jnp.zeros((d, A_NOPE), F32), w_kr, w_rot], axis=-1)
    kr2_tile = jnp.concatenate([jnp.zeros((d, A_NOPE), F32), w_rot, w_kr], axis=-1)
    w_in_p = jnp.concatenate(parts[:7] + [kr_tile, kr2_tile, parts[8]], axis=-1).astype(BF16)

    w_uq = p["w_uq"][l].reshape(Q_LORA, A_HEADS, A_QK)
    w_uq_rope = w_uq[:, :, A_NOPE:]
    w_uq_p = jnp.concatenate([w_uq, w_uq_rope[:, :, src] * sign[None, None, :]], axis=-1)
    w_uq_t = w_uq_p.reshape(Q_LORA, A_HEADS * LANES).T.astype(BF16)

    w_ukv = p["w_ukv"][l].reshape(KV_LORA, A_HEADS, A_NOPE + A_V)
    w_uk_p = jnp.concatenate([w_ukv[:, :, :A_NOPE], jnp.zeros((KV_LORA, A_HEADS, LANES - A_NOPE), F32)], axis=-1)
    w_uk_p = w_uk_p.reshape(KV_LORA, A_HEADS * LANES).astype(BF16)
    w_uv_t = w_ukv[:, :, A_NOPE:].reshape(KV_LORA, A_WIDTH).T.astype(BF16)

    ws = p["g_ws"][l]
    ws_cat = jnp.transpose(ws, (1, 0, 2)).reshape(CHUNK, G_HEADS * CHUNK).astype(BF16)
    bs_tab = jnp.repeat(p["g_bs"][l].T, G_HEAD_DIM, axis=1)
    return {
        "norm_g": p["norm_g"][l][None, :], "w_in": w_in_p, "w_uq_t": w_uq_t, "w_uk": w_uk_p, "w_uv_t": w_uv_t,
        "g_ln_g": jnp.tile(p["g_ln_g"][l], G_HEADS)[None, :], "ws_cat": ws_cat, "bs_tab": bs_tab,
        "q_a_g": p["q_a_g"][l][None, :], "kv_a_g": p["kv_a_g"][l][None, :],
        "w_fmix": p["w_fmix"][l].astype(BF16), "w_out": p["w_out"][l].astype(BF16),
    }


def _token_tile(n):
    return 512 if n % 512 == 0 else 256


def kernel(x, c, ctx, c_ctx, w_mod, b_mod, norm_g, w_in, w_fmix, g_ln_g, g_ws, g_bs,
           q_a_g, w_uq, kv_a_g, w_ukv, q_norm_g, k_norm_g, w_out):
    p = dict(w_in=w_in, w_fmix=w_fmix, g_ln_g=g_ln_g, g_ws=g_ws, g_bs=g_bs, q_a_g=q_a_g, w_uq=w_uq,
             kv_a_g=kv_a_g, w_ukv=w_ukv, norm_g=norm_g, w_out=w_out)
    depth = w_mod.shape[0]
    b, n, d = x.shape
    n_ctx = ctx.shape[1]
    src, sign = _rot_perm_sign()

    rows = -(-(b + 1) // 8) * 8
    cc = jnp.concatenate([c, c_ctx[None, :], jnp.zeros((rows - b - 1, d), F32)], axis=0)
    mod = _modulation(_silu(cc), w_mod, b_mod)

    cos64, sin64 = _dft_tables(F_GROUP_DIM)
    eye_g = jnp.eye(F_GROUPS, dtype=F32)
    bc = jnp.kron(eye_g, cos64).astype(BF16)
    bs = jnp.kron(eye_g, sin64).astype(BF16)
    jm = jnp.kron(jnp.eye(G_HEADS, dtype=F32), jnp.full((G_HEAD_DIM, G_HEAD_DIM), 1.0 / G_HEAD_DIM, F32)).astype(BF16)
    cos_x, sin_x = _rope_cos_sin(n)
    cos_y, sin_y = jnp.ones((n_ctx, A_ROPE), F32), jnp.zeros((n_ctx, A_ROPE), F32)
    q_scale = (A_QK ** -0.5) * math.log2(math.e)

    y, prev_x, prev_y = ctx, None, None
    for l in range(depth):
        lw = _layer_weights(l, p, src, sign)
        lw.update(jm=jm)
        shift, scale, gate = (mod[l, :, i * d:(i + 1) * d] for i in range(3))
        tabs_x, tabs_y = {}, {}
        for tabs, cs, sn in ((tabs_x, cos_x, sin_x), (tabs_y, cos_y, sin_y)):
            tabs["qt"] = (_head_tables(q_norm_g[l], cs, sn, src)[0] * q_scale).T
            tabs["k1"], tabs["k2"] = _head_tables(k_norm_g[l], cs, sn, src)

        score_bound = (1.02 * A_QK * q_scale) * jnp.max(jnp.abs(q_norm_g[l])) * jnp.max(jnp.abs(k_norm_g[l]))

        tm_y, tm_x = _token_tile(n_ctx), _token_tile(n)
        outs_y = _in_proj(y, scale[b:b + 1, None, :], shift[b:b + 1, None, :], lw, tabs_y, tm_y, prev_y)
        outs_x = _in_proj(x, scale[:b, None, :], shift[:b, None, :], lw, tabs_x, tm_x, prev_x)
        if l > 0:
            y, x = outs_y[0], outs_x[0]
            outs_y, outs_x = outs_y[1:], outs_x[1:]
        fin_y, sfg_y, gout_y, q_y, k_y, vt_y, sag_y = outs_y
        fin_x, sfg_x, gout_x, q_x, k_x, vt_x, sag_x = outs_x

        f_x = _pos_dft(fin_x, bc, bs)
        o_x = _attention(q_x, [k_x, k_y], [vt_x, vt_y], tm_x, score_bound)
        prev_x = (gate[:b, None, :], f_x, sfg_x, gout_x, o_x, sag_x, lw["w_fmix"], lw["w_out"])
        if l < depth - 1:
            f_y = _pos_dft(fin_y, bc, bs)
            o_y = _attention(q_y, [k_y], [vt_y], tm_y, score_bound)
            prev_y = (gate[b:b + 1, None, :], f_y, sfg_y, gout_y, o_y, sag_y, lw["w_fmix"], lw["w_out"])
    return _out_proj(x, *prev_x, 2 * _token_tile(n) if n % (2 * _token_tile(n)) == 0 else _token_tile(n))
```

```python
import functools
import math

import jax
import jax.numpy as jnp
from jax import lax
from jax.experimental import pallas as pl
from jax.experimental.pallas import tpu as pltpu

F32 = jnp.float32
BF16 = jnp.bfloat16

LANES = 128
VMEM_LIMIT_BYTES = 56 * 1024 * 1024

GRID_W = 64
F_GROUPS = 4
F_GROUP_DIM = 64
F_WIDTH = F_GROUPS * F_GROUP_DIM
G_HEADS = 4
G_HEAD_DIM = 64
G_WIDTH = G_HEADS * G_HEAD_DIM
CHUNK = 128
A_HEADS = 8
A_NOPE = 64
A_ROPE = 32
A_V = 64
A_QK = A_NOPE + A_ROPE
A_WIDTH = A_HEADS * A_V
Q_LORA = 256
KV_LORA = 128
ROPE_BASE = 10000.0
EPS = 1e-6
KEY_CHUNK = 256
MAX_SAFE_SHIFT = 60.0
HEAD_GROUP = 8
LOOKAHEAD = 2

OFF_FIN = 0
OFF_FGATE = OFF_FIN + F_WIDTH
OFF_GU = OFF_FGATE + F_WIDTH
OFF_GV = OFF_GU + G_WIDTH
OFF_GGATE = OFF_GV + G_WIDTH
OFF_CQ = OFF_GGATE + G_WIDTH
OFF_CKV = OFF_CQ + Q_LORA
OFF_KR = OFF_CKV + KV_LORA
OFF_KR2 = OFF_KR + LANES
OFF_AGATE = OFF_KR2 + LANES
IN_P = OFF_AGATE + A_WIDTH


def _silu(t):
    return t * (1.0 / (1.0 + jnp.exp(-t)))


def _dot(a, b):
    return jnp.dot(a, b, preferred_element_type=F32)


def _dot_hilo(t, w):
    hi = t.astype(BF16)
    lo = (t - hi.astype(F32)).astype(BF16)
    return _dot(hi, w) + _dot(lo, w)


def _mod_kernel(cc_ref, w_ref, b_ref, o_ref):
    o_ref[0] = lax.dot_general(cc_ref[...], w_ref[0], (((1,), (0,)), ((), ())),
                               precision=lax.Precision.HIGHEST,
                               preferred_element_type=F32) + b_ref[0]


def _modulation(cc, w_mod, b_mod):
    depth, d, d3 = w_mod.shape
    rows = cc.shape[0]
    tn = 1024
    return pl.pallas_call(
        _mod_kernel,
        grid=(depth, d3 // tn),
        in_specs=[pl.BlockSpec((rows, d), lambda l, j: (0, 0)),
                  pl.BlockSpec((1, d, tn), lambda l, j: (l, 0, j)),
                  pl.BlockSpec((1, 1, tn), lambda l, j: (l, 0, j))],
        out_specs=pl.BlockSpec((1, rows, tn), lambda l, j: (l, 0, j)),
        out_shape=jax.ShapeDtypeStruct((depth, rows, d3), F32),
        compiler_params=pltpu.CompilerParams(vmem_limit_bytes=VMEM_LIMIT_BYTES),
        name="modulation",
    )(cc, w_mod, b_mod.reshape(depth, 1, d3))


def _in_proj_kernel(*refs, fused):
    if fused:
        x = _out_proj_tile(*refs[:9])
        refs = refs[9:]
    else:
        x = refs[0][0]
        refs = refs[1:]
    (sc_ref, sh_ref, ng_ref, win_ref, jm_ref, gln_ref, ws_ref, bsb_ref, qag_ref, wuqt_ref, kvag_ref,
     wuk_ref, wuvt_ref, tqt_ref, tk1_ref, tk2_ref) = refs[:16]
    outs = refs[16:]
    if fused:
        outs[0][0] = x
        outs = outs[1:]
    fin_ref, sfg_ref, gout_ref, q_ref, k_ref, vt_ref, sag_ref = outs
    tm = x.shape[0]
    xn = x * lax.rsqrt(jnp.mean(x * x, axis=-1, keepdims=True) + EPS) * ng_ref[...]
    hb = (xn * (1.0 + sc_ref[0]) + sh_ref[0]).astype(BF16)

    def proj(off, width):
        return _dot(hb, win_ref[:, off:off + width])

    g_v = proj(OFF_GV, G_WIDTH)
    c_q = proj(OFF_CQ, Q_LORA)
    ckv_kr = proj(OFF_CKV, KV_LORA + LANES)
    c_kv, kr = ckv_kr[:, 0:KV_LORA], ckv_kr[:, KV_LORA:]
    kr2 = proj(OFF_KR2, LANES) * tk2_ref[...]
    fin_ref[0] = proj(OFF_FIN, F_WIDTH).astype(BF16)
    jm = jm_ref[...]
    dv = g_v - _dot_hilo(g_v, jm)
    sfg_ref[0] = _silu(proj(OFF_FGATE, F_WIDTH)).astype(BF16)
    var = _dot_hilo(dv * dv, jm)
    g_u = proj(OFF_GU, G_WIDTH)
    g_gate = proj(OFF_GGATE, G_WIDTH)

    cqn = c_q * lax.rsqrt(jnp.mean(c_q * c_q, axis=-1, keepdims=True) + EPS) * qag_ref[...]
    qt_all = _dot(wuqt_ref[...], cqn.T.astype(BF16))
    tqt = tqt_ref[...]
    for h in range(A_HEADS):
        t = qt_all[h * LANES:(h + 1) * LANES]
        ss = jnp.sum(t[0:A_QK] * t[0:A_QK], axis=0, keepdims=True)
        q_ref[0, h] = (t * lax.rsqrt(ss * (1.0 / A_QK) + EPS) * tqt).astype(BF16)

    lane = lax.broadcasted_iota(jnp.int32, (1, LANES), 1)
    lane_mask = (lane < A_QK).astype(F32)
    ckvn = c_kv * lax.rsqrt(jnp.mean(c_kv * c_kv, axis=-1, keepdims=True) + EPS) * kvag_ref[...]
    ckvn_b = ckvn.astype(BF16)
    tk1 = tk1_ref[...]
    for h in range(0, A_HEADS, 2):
        t_pair = _dot(ckvn_b, wuk_ref[:, h * LANES:(h + 2) * LANES])
        for g in range(2):
            t = t_pair[:, g * LANES:(g + 1) * LANES] + kr
            ss = jnp.sum(t * t * lane_mask, axis=-1, keepdims=True)
            k_ref[0, h + g] = ((t * tk1 + kr2) * lax.rsqrt(ss * (1.0 / A_QK) + EPS)).astype(BF16)
    vt = _dot(wuvt_ref[...], ckvn.T.astype(BF16))
    vt_ref[0] = vt.astype(BF16).reshape(A_HEADS, A_V, tm)

    sag_ref[0] = _silu(proj(OFF_AGATE, A_WIDTH)).astype(BF16)

    vln = dv * lax.rsqrt(var + EPS) * gln_ref[...]
    head_of_lane = lax.broadcasted_iota(jnp.int32, (CHUNK, G_WIDTH), 1) // G_HEAD_DIM
    for c in range(tm // CHUNK):
        rows = slice(c * CHUNK, (c + 1) * CHUNK)
        vc = vln[rows]
        stacked = jnp.concatenate(
            [jnp.where(head_of_lane == h, vc, 0.0).astype(BF16) for h in range(G_HEADS)], axis=0)
        mixed = _dot(ws_ref[...], stacked) + bsb_ref[...]
        gout_ref[0, rows, :] = (g_u[rows] * mixed * _silu(g_gate[rows])).astype(BF16)


def _in_proj(x, scale, shift, lw, tabs, tm, prev=None):
    b, n, d = x.shape
    mod_map = (lambda bi, i: (bi, 0, 0)) if scale.shape[0] > 1 else (lambda bi, i: (0, 0, 0))
    const2 = lambda bi, i: (0, 0)
    row_tab = lambda bi, i: (i, 0)
    tok = lambda bi, i: (bi, i, 0)
    full = lambda a: pl.BlockSpec(a.shape, const2)
    if prev is None:
        args, in_specs = [x], [pl.BlockSpec((1, tm, d), tok)]
    else:
        args, in_specs = _out_proj_operands(x, *prev, tm)
    weights = [lw["norm_g"], lw["w_in"], lw["jm"], lw["g_ln_g"], lw["ws_cat"], lw["bs_tab"], lw["q_a_g"],
               lw["w_uq_t"], lw["kv_a_g"], lw["w_uk"], lw["w_uv_t"]]
    args += [scale, shift] + weights + [tabs["qt"], tabs["k1"], tabs["k2"]]
    in_specs += [pl.BlockSpec((1, 1, d), mod_map), pl.BlockSpec((1, 1, d), mod_map)]
    in_specs += [full(a) for a in weights]
    in_specs += [pl.BlockSpec((LANES, tm), lambda bi, i: (0, i)),
                 pl.BlockSpec((tm, LANES), row_tab), pl.BlockSpec((tm, LANES), row_tab)]
    out_shape = [
        jax.ShapeDtypeStruct((b, n, F_WIDTH), BF16),
        jax.ShapeDtypeStruct((b, n, F_WIDTH), BF16),
        jax.ShapeDtypeStruct((b, n, G_WIDTH), BF16),
        jax.ShapeDtypeStruct((b, A_HEADS, LANES, n), BF16),
        jax.ShapeDtypeStruct((b, A_HEADS, n, LANES), BF16),
        jax.ShapeDtypeStruct((b, A_HEADS, A_V, n), BF16),
        jax.ShapeDtypeStruct((b, n, A_WIDTH), BF16),
    ]
    out_specs = [
        pl.BlockSpec((1, tm, F_WIDTH), tok),
        pl.BlockSpec((1, tm, F_WIDTH), tok),
        pl.BlockSpec((1, tm, G_WIDTH), tok),
        pl.BlockSpec((1, A_HEADS, LANES, tm), lambda bi, i: (bi, 0, 0, i)),
        pl.BlockSpec((1, A_HEADS, tm, LANES), lambda bi, i: (bi, 0, i, 0)),
        pl.BlockSpec((1, A_HEADS, A_V, tm), lambda bi, i: (bi, 0, 0, i)),
        pl.BlockSpec((1, tm, A_WIDTH), tok),
    ]
    if prev is not None:
        out_shape = [jax.ShapeDtypeStruct((b, n, d), F32)] + out_shape
        out_specs = [pl.BlockSpec((1, tm, d), tok)] + out_specs
    return pl.pallas_call(
        functools.partial(_in_proj_kernel, fused=prev is not None),
        grid=(b, n // tm),
        in_specs=in_specs,
        out_specs=out_specs,
        out_shape=out_shape,
        compiler_params=pltpu.CompilerParams(
            dimension_semantics=("arbitrary", "arbitrary"), vmem_limit_bytes=VMEM_LIMIT_BYTES),
        name="in_proj_fused" if prev is not None else "in_proj",
    )(*args)


DFT_BLOCK = 16
DFT_T = 8


def _dft_stage1_kernel(wc_ref, ws_ref, u_ref, ct_ref, st_ref, zr_ref, zi_ref):
    r, blk, w = u_ref.shape[1:]
    u32 = u_ref[0].astype(F32)
    zr, zi = [], []
    for t0 in range(0, blk, DFT_T):
        part = lambda a: a[:, t0:t0 + DFT_T, :].reshape(r * DFT_T, w)
        u = part(u32).astype(BF16)
        yr = _dot(wc_ref[...], u)
        yi = _dot(ws_ref[...], u)
        ct, st = part(ct_ref[...]), part(st_ref[...])
        zr.append((yr * ct + yi * st).reshape(r, DFT_T, w))
        zi.append((yi * ct - yr * st).reshape(r, DFT_T, w))
    zr_ref[0] = jnp.concatenate(zr, axis=1).astype(BF16)
    zi_ref[0] = jnp.concatenate(zi, axis=1).astype(BF16)


def _dft_stage2_kernel(zr_ref, zi_ref, d_ref, bc_ref, bs_ref, o_ref, *, norm):
    groups = o_ref.shape[2]
    xs = []
    for j in range(groups):
        rows = slice(j * GRID_W, (j + 1) * GRID_W)
        z = jnp.concatenate([zr_ref[0, rows, :], zi_ref[0, rows, :]], axis=0)
        xs.append(_dot(d_ref[...], z))
    xr = jnp.concatenate([x[0:GRID_W] for x in xs], axis=0).astype(BF16)
    xi = jnp.concatenate([x[GRID_W:] for x in xs], axis=0).astype(BF16)
    f = (_dot(xr, bc_ref[...]) + _dot(xi, bs_ref[...])) * norm
    for j in range(groups):
        o_ref[0, :, j, :] = f[j * GRID_W:(j + 1) * GRID_W]


def _pos_dft_factored(u, bc, bs):
    b, n, w = u.shape
    r = n // GRID_W
    i = jnp.arange(r * DFT_T, dtype=jnp.int32)
    k1, t = i // DFT_T, i % DFT_T
    ang_r = ((k1[:, None] * k1[None, :]) % r).astype(F32) * (2.0 * math.pi / r)
    same_t = t[:, None] == t[None, :]
    wc = jnp.where(same_t, jnp.cos(ang_r), 0.0).astype(BF16)
    ws = jnp.where(same_t, -jnp.sin(ang_r), 0.0).astype(BF16)
    ang = (jnp.arange(r, dtype=jnp.int32)[:, None] * jnp.arange(GRID_W, dtype=jnp.int32)[None, :]
           ).astype(F32) * (2.0 * math.pi / n)
    ct = jnp.broadcast_to(jnp.cos(ang)[:, :, None], (r, GRID_W, w))
    st = jnp.broadcast_to(jnp.sin(ang)[:, :, None], (r, GRID_W, w))
    blk = lambda j, bi: (bi, 0, j, 0)
    zr, zi = pl.pallas_call(
        _dft_stage1_kernel,
        grid=(GRID_W // DFT_BLOCK, b),
        in_specs=[pl.BlockSpec(wc.shape, lambda j, bi: (0, 0)),
                  pl.BlockSpec(ws.shape, lambda j, bi: (0, 0)),
                  pl.BlockSpec((1, r, DFT_BLOCK, w), blk),
                  pl.BlockSpec((r, DFT_BLOCK, w), lambda j, bi: (0, j, 0)),
                  pl.BlockSpec((r, DFT_BLOCK, w), lambda j, bi: (0, j, 0))],
        out_specs=[pl.BlockSpec((1, r, DFT_BLOCK, w), blk), pl.BlockSpec((1, r, DFT_BLOCK, w), blk)],
        out_shape=[jax.ShapeDtypeStruct((b, r, GRID_W, w), BF16)] * 2,
        compiler_params=pltpu.CompilerParams(
            dimension_semantics=("arbitrary", "arbitrary"), vmem_limit_bytes=VMEM_LIMIT_BYTES),
        name="dft_stage1",
    )(wc, ws, u.reshape(b, r, GRID_W, w), ct, st)

    cos64, sin64 = _dft_tables(GRID_W)
    d = jnp.concatenate([jnp.concatenate([cos64, sin64], axis=1),
                         jnp.concatenate([-sin64, cos64], axis=1)], axis=0).astype(BF16)
    groups = min(32, r)
    f = pl.pallas_call(
        functools.partial(_dft_stage2_kernel, norm=1.0 / math.sqrt(n * F_GROUP_DIM)),
        grid=(b, r // groups),
        in_specs=[pl.BlockSpec((1, groups * GRID_W, w), lambda bi, g: (bi, g, 0)),
                  pl.BlockSpec((1, groups * GRID_W, w), lambda bi, g: (bi, g, 0)),
                  pl.BlockSpec(d.shape, lambda bi, g: (0, 0)),
                  pl.BlockSpec(bc.shape, lambda bi, g: (0, 0)),
                  pl.BlockSpec(bs.shape, lambda bi, g: (0, 0))],
        out_specs=pl.BlockSpec((1, GRID_W, groups, w), lambda bi, g: (bi, 0, g, 0)),
        out_shape=jax.ShapeDtypeStruct((b, GRID_W, r, w), F32),
        compiler_params=pltpu.CompilerParams(
            dimension_semantics=("arbitrary", "arbitrary"), vmem_limit_bytes=VMEM_LIMIT_BYTES),
        name="dft_stage2",
    )(zr.reshape(b, n, w), zi.reshape(b, n, w), d, bc, bs)
    return f.reshape(b, n, w)


def _dft_dense_kernel(cn_ref, sn_ref, u_ref, bc_ref, bs_ref, o_ref, *, norm):
    u = u_ref[0]
    pc = _dot(cn_ref[...], u).astype(BF16)
    ps = _dot(sn_ref[...], u).astype(BF16)
    o_ref[0] = (_dot(pc, bc_ref[...]) - _dot(ps, bs_ref[...])) * norm


def _pos_dft_dense(u, bc, bs):
    b, n, w = u.shape
    cos_n, sin_n = _dft_tables(n)
    const2 = lambda bi: (0, 0)
    return pl.pallas_call(
        functools.partial(_dft_dense_kernel, norm=1.0 / math.sqrt(n * F_GROUP_DIM)),
        grid=(b,),
        in_specs=[pl.BlockSpec((n, n), const2), pl.BlockSpec((n, n), const2),
                  pl.BlockSpec((1, n, w), lambda bi: (bi, 0, 0)),
                  pl.BlockSpec(bc.shape, const2), pl.BlockSpec(bs.shape, const2)],
        out_specs=pl.BlockSpec((1, n, w), lambda bi: (bi, 0, 0)),
        out_shape=jax.ShapeDtypeStruct((b, n, w), F32),
        compiler_params=pltpu.CompilerParams(
            dimension_semantics=("arbitrary",), vmem_limit_bytes=VMEM_LIMIT_BYTES),
        name="dft_dense",
    )(cos_n.astype(BF16), sin_n.astype(BF16), u, bc, bs)


def _pos_dft(u, bc, bs):
    r = u.shape[1] // GRID_W
    if r % 16 == 0:
        return _pos_dft_factored(u, bc, bs)
    return _pos_dft_dense(u, bc, bs)


def _key_chunks(k_refs):
    chunks = []
    for seg, k_ref in enumerate(k_refs):
        for c in range(k_ref.shape[2] // KEY_CHUNK):
            chunks.append((seg, c, len(chunks)))
    return chunks


def _rows(c):
    return slice(c * KEY_CHUNK, (c + 1) * KEY_CHUNK)


def _sublane_partial_sum(p_t):
    return jnp.sum(p_t.reshape(p_t.shape[0] // 8, 8, p_t.shape[1]), axis=0)


def _attention_kernel(*refs, nseg):
    qt_ref, k_refs, vt_refs = refs[0], refs[1:1 + nseg], refs[1 + nseg:1 + 2 * nseg]
    o_ref, s0_ref, s1_ref, ot_ref = refs[1 + 2 * nseg:]
    heads, tq = qt_ref.shape[1], qt_ref.shape[3]
    chunks = _key_chunks(k_refs)
    s_bufs = (s0_ref, s1_ref)

    def phase(h_prod, h_cons, m_cons, prod_buf):
        m_part, acc, l_part = None, None, None
        for seg, c, row in chunks:
            if h_prod is not None:
                s_t = _dot(k_refs[seg][0, h_prod, _rows(c), :], qt_ref[0, h_prod])
                s_bufs[prod_buf][_rows(row), :] = s_t
                pm = jnp.max(s_t.reshape(KEY_CHUNK // 8, 8, tq), axis=0)
                m_part = pm if m_part is None else jnp.maximum(m_part, pm)
            if h_cons is not None:
                p_t = jnp.exp2(s_bufs[1 - prod_buf][_rows(row), :] - m_cons)
                d = _dot(vt_refs[seg][0, h_cons, :, _rows(c)], p_t.astype(BF16))
                acc = d if acc is None else acc + d
                l_part = _sublane_partial_sum(p_t) if l_part is None else l_part + _sublane_partial_sum(p_t)
        if h_cons is not None:
            ot_ref[pl.ds(pl.multiple_of(h_cons * A_V, A_V), A_V), :] = (
                acc / jnp.sum(l_part, axis=0, keepdims=True))
        return None if h_prod is None else jnp.max(m_part, axis=0, keepdims=True)

    def head_pair(j, m):
        m = phase(2 * j + 1, 2 * j, m, 1)
        return phase(2 * j + 2, 2 * j + 1, m, 0)

    m = phase(0, None, None, 0)
    m = lax.fori_loop(0, (heads - 2) // 2, head_pair, m)
    m = phase(heads - 1, heads - 2, m, 1)
    phase(None, heads - 1, m, 0)
    o_ref[0] = ot_ref[...].T.astype(BF16)


def _attention_bounded_kernel(*refs, nseg):
    shift_ref, qt_ref, k_refs, vt_refs = refs[0], refs[1], refs[2:2 + nseg], refs[2 + nseg:2 + 2 * nseg]
    o_ref, ot_ref = refs[2 + 2 * nseg:]
    heads = qt_ref.shape[1]
    chunks = _key_chunks(k_refs)
    shift = shift_ref[0, 0]

    def head_group(j, carry):
        items = [(j * HEAD_GROUP + g, chunk) for g in range(HEAD_GROUP) for chunk in chunks]
        scores, acc, l_part = [], None, None
        for i in range(len(items) + LOOKAHEAD):
            if i < len(items):
                h, (seg, c, _) = items[i]
                scores.append(_dot(k_refs[seg][0, h, _rows(c), :], qt_ref[0, h]))
            if i >= LOOKAHEAD:
                h, (seg, c, row) = items[i - LOOKAHEAD]
                p_t = jnp.exp2(scores.pop(0) - shift)
                d = _dot(vt_refs[seg][0, h, :, _rows(c)], p_t.astype(BF16))
                acc = d if row == 0 else acc + d
                l_part = _sublane_partial_sum(p_t) if row == 0 else l_part + _sublane_partial_sum(p_t)
                if row == len(chunks) - 1:
                    ot_ref[pl.ds(pl.multiple_of(h * A_V, A_V), A_V), :] = (
                        acc / jnp.sum(l_part, axis=0, keepdims=True))
        return carry

    lax.fori_loop(0, heads // HEAD_GROUP, head_group, 0)
    o_ref[0] = ot_ref[...].T.astype(BF16)


def _attention(qt, ks, vts, tq, score_bound):
    b, heads, _, n = qt.shape
    nseg = len(ks)
    n_keys = sum(k.shape[2] for k in ks)
    kvmap = lambda bi, qi: (bi, 0, 0, 0)
    q_spec = pl.BlockSpec((1, heads, LANES, tq), lambda bi, qi: (bi, 0, 0, qi))
    o_spec = pl.BlockSpec((1, tq, heads * A_V), lambda bi, qi: (bi, qi, 0))
    o_shape = jax.ShapeDtypeStruct((b, n, heads * A_V), BF16)
    params = pltpu.CompilerParams(dimension_semantics=("arbitrary", "arbitrary"),
                                  vmem_limit_bytes=VMEM_LIMIT_BYTES)

    def kv_specs(**kw):
        return ([pl.BlockSpec((1,) + k.shape[1:], kvmap, **kw) for k in ks]
                + [pl.BlockSpec((1,) + vt.shape[1:], kvmap, **kw) for vt in vts])

    def bounded(qt, *kv):
        return pl.pallas_call(
            functools.partial(_attention_bounded_kernel, nseg=nseg),
            grid=(b, n // tq),
            in_specs=[pl.BlockSpec(memory_space=pltpu.SMEM), q_spec] + kv_specs(),
            out_specs=o_spec,
            out_shape=o_shape,
            scratch_shapes=[pltpu.VMEM((heads * A_V, tq), F32)],
            compiler_params=params,
            name="attention_bounded",
        )(jnp.reshape(score_bound, (1, 1)), qt, *kv)

    def row_max(qt, *kv):
        return pl.pallas_call(
            functools.partial(_attention_kernel, nseg=nseg),
            grid=(b, n // tq),
            in_specs=[q_spec] + kv_specs(pipeline_mode=pl.Buffered(1)),
            out_specs=o_spec,
            out_shape=o_shape,
            scratch_shapes=[pltpu.VMEM((n_keys, tq), F32), pltpu.VMEM((n_keys, tq), F32),
                            pltpu.VMEM((heads * A_V, tq), F32)],
            compiler_params=params,
            name="attention",
        )(qt, *kv)

    return lax.cond(score_bound <= MAX_SAFE_SHIFT, bounded, row_max, qt, *ks, *vts)


def _out_proj_tile(x_ref, gate_ref, f_ref, sfg_ref, gout_ref, o_ref, sag_ref, wf_ref, wo_ref):
    f_out = (_dot(f_ref[0].astype(BF16), wf_ref[...]) * sfg_ref[0].astype(F32)).astype(BF16)
    o_g = (o_ref[0].astype(F32) * sag_ref[0].astype(F32)).astype(BF16)
    y = (_dot(f_out, wo_ref[0:F_WIDTH, :])
         + _dot(gout_ref[0], wo_ref[F_WIDTH:F_WIDTH + G_WIDTH, :])
         + _dot(o_g, wo_ref[F_WIDTH + G_WIDTH:, :]))
    return x_ref[0] + gate_ref[0] * y


def _out_proj_kernel(*refs):
    refs[-1][0] = _out_proj_tile(*refs[:-1])


def _out_proj_operands(x, gate, f, sfg, gout, o, sag, w_fmix, w_out, tm):
    d = x.shape[2]
    tok = lambda bi, i: (bi, i, 0)
    mod_map = (lambda bi, i: (bi, 0, 0)) if gate.shape[0] > 1 else (lambda bi, i: (0, 0, 0))
    const2 = lambda bi, i: (0, 0)
    in_specs = [pl.BlockSpec((1, tm, d), tok),
                pl.BlockSpec((1, 1, d), mod_map),
                pl.BlockSpec((1, tm, F_WIDTH), tok),
                pl.BlockSpec((1, tm, F_WIDTH), tok),
                pl.BlockSpec((1, tm, G_WIDTH), tok),
                pl.BlockSpec((1, tm, A_WIDTH), tok),
                pl.BlockSpec((1, tm, A_WIDTH), tok),
                pl.BlockSpec(w_fmix.shape, const2),
                pl.BlockSpec(w_out.shape, const2)]
    return [x, gate, f, sfg, gout, o, sag, w_fmix, w_out], in_specs


def _out_proj(x, gate, f, sfg, gout, o, sag, w_fmix, w_out, tm):
    b, n, d = x.shape
    args, in_specs = _out_proj_operands(x, gate, f, sfg, gout, o, sag, w_fmix, w_out, tm)
    return pl.pallas_call(
        _out_proj_kernel,
        grid=(b, n // tm),
        in_specs=in_specs,
        out_specs=pl.BlockSpec((1, tm, d), lambda bi, i: (bi, i, 0)),
        out_shape=jax.ShapeDtypeStruct((b, n, d), F32),
        compiler_params=pltpu.CompilerParams(
            dimension_semantics=("arbitrary", "arbitrary"), vmem_limit_bytes=VMEM_LIMIT_BYTES),
        name="out_proj",
    )(*args)


def _rot_perm_sign():
    q = A_ROPE // 4
    src, sign = [], []
    for j in range(A_ROPE):
        blk, r = divmod(j, q)
        if blk % 2 == 0:
            src.append((blk + 1) * q + r)
            sign.append(-1.0)
        else:
            src.append((blk - 1) * q + r)
            sign.append(1.0)
    return jnp.array(src, jnp.int32), jnp.array(sign, F32)


def _dft_tables(n):
    idx = (jnp.arange(n, dtype=jnp.int32)[:, None] * jnp.arange(n, dtype=jnp.int32)[None, :]) % n
    ang = idx.astype(F32) * (2.0 * math.pi / n)
    return jnp.cos(ang), jnp.sin(ang)


def _rope_cos_sin(n):
    rows = n // GRID_W
    row = jnp.repeat(jnp.arange(rows, dtype=F32), GRID_W)
    col = jnp.tile(jnp.arange(GRID_W, dtype=F32), rows)
    half = A_ROPE // 2
    inv = ROPE_BASE ** (-jnp.arange(0, half, 2, dtype=F32) / half)
    ang_r = row[:, None] * inv[None, :]
    ang_c = col[:, None] * inv[None, :]
    ang = jnp.concatenate([ang_r, ang_r, ang_c, ang_c], axis=-1)
    return jnp.cos(ang), jnp.sin(ang)


def _head_tables(gain, cos, sin, src):
    n = cos.shape[0]
    g_nope = jnp.broadcast_to(gain[:A_NOPE][None, :], (n, A_NOPE))
    a = gain[A_NOPE:][None, :] * cos
    b = gain[A_NOPE:][src][None, :] * sin
    tab1 = jnp.concatenate([g_nope, a, b], axis=-1)
    tab2 = jnp.concatenate([jnp.zeros((n, A_NOPE), F32), b, a], axis=-1)
    return tab1, tab2


def _layer_weights(l, p, src, sign):
    w_in = p["w_in"][l]
    d = w_in.shape[0]
    sizes = (F_WIDTH, F_WIDTH, G_WIDTH, G_WIDTH, G_WIDTH, Q_LORA, KV_LORA, A_ROPE, A_WIDTH)
    parts, start = [], 0
    for s in sizes:
        parts.append(w_in[:, start:start + s])
        start += s
    w_kr = parts[7]
    w_rot = w_kr[:, src] * sign[None, :]
    kr_tile = jnp.concatenate([jnp.zeros((d, A_NOPE), F32), w_kr, w_rot], axis=-1)
    kr2_tile = jnp.concatenate([jnp.zeros((d, A_NOPE), F32), w_rot, w_kr], axis=-1)
    w_in_p = jnp.concatenate(parts[:7] + [kr_tile, kr2_tile, parts[8]], axis=-1).astype(BF16)

    w_uq = p["w_uq"][l].reshape(Q_LORA, A_HEADS, A_QK)
    w_uq_rope = w_uq[:, :, A_NOPE:]
    w_uq_p = jnp.concatenate([w_uq, w_uq_rope[:, :, src] * sign[None, None, :]], axis=-1)
    w_uq_t = w_uq_p.reshape(Q_LORA, A_HEADS * LANES).T.astype(BF16)

    w_ukv = p["w_ukv"][l].reshape(KV_LORA, A_HEADS, A_NOPE + A_V)
    w_uk_p = jnp.concatenate([w_ukv[:, :, :A_NOPE], jnp.zeros((KV_LORA, A_HEADS, LANES - A_NOPE), F32)], axis=-1)
    w_uk_p = w_uk_p.reshape(KV_LORA, A_HEADS * LANES).astype(BF16)
    w_uv_t = w_ukv[:, :, A_NOPE:].reshape(KV_LORA, A_WIDTH).T.astype(BF16)

    ws = p["g_ws"][l]
    ws_cat = jnp.transpose(ws, (1, 0, 2)).reshape(CHUNK, G_HEADS * CHUNK).astype(BF16)
    bs_tab = jnp.repeat(p["g_bs"][l].T, G_HEAD_DIM, axis=1)
    return {
        "norm_g": p["norm_g"][l][None, :], "w_in": w_in_p, "w_uq_t": w_uq_t, "w_uk": w_uk_p, "w_uv_t": w_uv_t,
        "g_ln_g": jnp.tile(p["g_ln_g"][l], G_HEADS)[None, :], "ws_cat": ws_cat, "bs_tab": bs_tab,
        "q_a_g": p["q_a_g"][l][None, :], "kv_a_g": p["kv_a_g"][l][None, :],
        "w_fmix": p["w_fmix"][l].astype(BF16), "w_out": p["w_out"][l].astype(BF16),
    }


def _token_tile(n):
    return 512 if n % 512 == 0 else 256


def kernel(x, c, ctx, c_ctx, w_mod, b_mod, norm_g, w_in, w_fmix, g_ln_g, g_ws, g_bs,
           q_a_g, w_uq, kv_a_g, w_ukv, q_norm_g, k_norm_g, w_out):
    p = dict(w_in=w_in, w_fmix=w_fmix, g_ln_g=g_ln_g, g_ws=g_ws, g_bs=g_bs, q_a_g=q_a_g, w_uq=w_uq,
             kv_a_g=kv_a_g, w_ukv=w_ukv, norm_g=norm_g, w_out=w_out)
    depth = w_mod.shape[0]
    b, n, d = x.shape
    n_ctx = ctx.shape[1]
    src, sign = _rot_perm_sign()

    rows = -(-(b + 1) // 8) * 8
    cc = jnp.concatenate([c, c_ctx[None, :], jnp.zeros((rows - b - 1, d), F32)], axis=0)
    mod = _modulation(_silu(cc), w_mod, b_mod)

    cos64, sin64 = _dft_tables(F_GROUP_DIM)
    eye_g = jnp.eye(F_GROUPS, dtype=F32)
    bc = jnp.kron(eye_g, cos64).astype(BF16)
    bs = jnp.kron(eye_g, sin64).astype(BF16)
    jm = jnp.kron(jnp.eye(G_HEADS, dtype=F32), jnp.full((G_HEAD_DIM, G_HEAD_DIM), 1.0 / G_HEAD_DIM, F32)).astype(BF16)
    cos_x, sin_x = _rope_cos_sin(n)
    cos_y, sin_y = jnp.ones((n_ctx, A_ROPE), F32), jnp.zeros((n_ctx, A_ROPE), F32)
    q_scale = (A_QK ** -0.5) * math.log2(math.e)

    y, prev_x, prev_y = ctx, None, None
    for l in range(depth):
        lw = _layer_weights(l, p, src, sign)
        lw.update(jm=jm)
        shift, scale, gate = (mod[l, :, i * d:(i + 1) * d] for i in range(3))
        tabs_x, tabs_y = {}, {}
        for tabs, cs, sn in ((tabs_x, cos_x, sin_x), (tabs_y, cos_y, sin_y)):
            tabs["qt"] = (_head_tables(q_norm_g[l], cs, sn, src)[0] * q_scale).T
            tabs["k1"], tabs["k2"] = _head_tables(k_norm_g[l], cs, sn, src)

        score_bound = (1.02 * A_QK * q_scale) * jnp.max(jnp.abs(q_norm_g[l])) * jnp.max(jnp.abs(k_norm_g[l]))

        tm_y, tm_x = _token_tile(n_ctx), _token_tile(n)
        outs_y = _in_proj(y, scale[b:b + 1, None, :], shift[b:b + 1, None, :], lw, tabs_y, tm_y, prev_y)
        outs_x = _in_proj(x, scale[:b, None, :], shift[:b, None, :], lw, tabs_x, tm_x, prev_x)
        if l > 0:
            y, x = outs_y[0], outs_x[0]
            outs_y, outs_x = outs_y[1:], outs_x[1:]
        fin_y, sfg_y, gout_y, q_y, k_y, vt_y, sag_y = outs_y
        fin_x, sfg_x, gout_x, q_x, k_x, vt_x, sag_x = outs_x

        f_x = _pos_dft(fin_x, bc, bs)
        o_x = _attention(q_x, [k_x, k_y], [vt_x, vt_y], tm_x, score_bound)
        prev_x = (gate[:b, None, :], f_x, sfg_x, gout_x, o_x, sag_x, lw["w_fmix"], lw["w_out"])
        if l < depth - 1:
            f_y = _pos_dft(fin_y, bc, bs)
            o_y = _attention(q_y, [k_y], [vt_y], tm_y, score_bound)
            prev_y = (gate[b:b + 1, None, :], f_y, sfg_y, gout_y, o_y, sag_y, lw["w_fmix"], lw["w_out"])
    return _out_proj(x, *prev_x, 2 * _token_tile(n) if n % (2 * _token_tile(n)) == 0 else _token_tile(n))
```
